```python
import math
import jax, jax.numpy as jnp
from jax import lax
import numpy as np

D_MODEL = 1024
BATCH = 8
SEQ = 4096
DEPTH = 2
DEC_BATCH = 32
DEC_SEQ = 4
PAST_LEN = 16384
PAGE_SIZE = 128

N_MIXERS = 2
N_RWKV_LAYERS = (DEPTH + 1) // 2
N_ATTN_LAYERS = DEPTH // 2
RK_HEAD = 64
RK_HEADS = D_MODEL // RK_HEAD
DECAY_LORA = 64
ICLR_LORA = 64
GATE_LORA = 160
GN_EPS = 64e-5
AT_HEAD_DIM = 64
AT_HEADS = D_MODEL // (2 * AT_HEAD_DIM)
ROT_DIM = AT_HEAD_DIM // 4
ROPE_THETA = 500000.0
Q_BLOCK = 128
ATTN_SCALE = AT_HEAD_DIM ** -0.5
NEG_INF = -1e30
D_FF = 2816
N_EXPERTS = 8
TOP_K = 2
D_FF_EXPERT = 1408
RMS_EPS = 1e-6

kernel_name = "rwkv7_diffattn_hybrid_step"


def rmsnorm(x, g):
    x32 = x.astype(jnp.float32)
    y = x32 * lax.rsqrt(jnp.mean(x32 * x32, axis=-1, keepdims=True) + RMS_EPS)
    return y.astype(x.dtype) * g


def rwkv7_time_mix(xn, shift_prev, s0, mix, wr, wk, wv, w0, w1, w2, a0, a1, a2,
                   g1, g2, k_k, k_a, r_k, lnx_g, lnx_b, wo):
    B, T, D = xn.shape
    prev = jnp.concatenate([shift_prev[:, None, :].astype(xn.dtype), xn[:, :-1]], axis=1)
    dx = prev - xn
    xr, xw, xk, xv, xa, xg = (xn + dx * mix[c] for c in range(6))
    r = xr @ wr
    k = xk @ wk
    v = xv @ wv
    w_log = -jax.nn.softplus(-(w0 + jnp.tanh(xw @ w1) @ w2)) - 0.5
    a = jax.nn.sigmoid(a0 + (xa @ a1) @ a2)
    g = jax.nn.sigmoid(xg @ g1) @ g2
    heads = lambda t: t.reshape(B, T, RK_HEADS, RK_HEAD).astype(jnp.float32)
    kk = heads(k * k_k)
    kk = kk / jnp.maximum(jnp.linalg.norm(kk, axis=-1, keepdims=True), 1e-12)
    k = k * (1.0 + (a - 1.0) * k_a)
    r_h, k_h, v_h, a_h = heads(r), heads(k), heads(v), heads(a)
    decay = jnp.exp(-jnp.exp(heads(w_log)))

    def step(S, inp):
        r_t, w_t, k_t, v_t, kk_t, a_t = inp
        sa = jnp.einsum('bhvk,bhk->bhv', S, -kk_t)
        S = (S * w_t[:, :, None, :]
             + sa[..., None] * (kk_t * a_t)[:, :, None, :]
             + v_t[..., None] * k_t[:, :, None, :])
        return S, jnp.einsum('bhvk,bhk->bhv', S, r_t)

    tm = lambda t: jnp.swapaxes(t, 0, 1)
    S_T, y = lax.scan(step, s0.astype(jnp.float32),
                      tuple(tm(t) for t in (r_h, decay, k_h, v_h, kk, a_h)))
    y = tm(y)
    mu = jnp.mean(y, axis=-1, keepdims=True)
    var = jnp.mean(jnp.square(y - mu), axis=-1, keepdims=True)
    y = ((y - mu) * lax.rsqrt(var + GN_EPS)).reshape(B, T, D).astype(xn.dtype) * lnx_g + lnx_b
    bonus = (jnp.sum(r_h * k_h * r_k, axis=-1, keepdims=True) * v_h).reshape(B, T, D).astype(xn.dtype)
    out = ((y + bonus) * g) @ wo
    return out, xn[:, -1], S_T.astype(s0.dtype)


def partial_rope(x, pos):
    half = ROT_DIM // 2
    inv = ROPE_THETA ** (-jnp.arange(0, ROT_DIM, 2, dtype=jnp.float32) / ROT_DIM)
    ang = pos.astype(jnp.float32)[:, None] * inv[None, :]
    cos = jnp.cos(ang)[None, :, None, None, :]
    sin = jnp.sin(ang)[None, :, None, None, :]
    xr = x[..., :ROT_DIM].astype(jnp.float32)
    x1, x2 = xr[..., :half], xr[..., half:]
    rot = jnp.concatenate([x1 * cos - x2 * sin, x2 * cos + x1 * sin], axis=-1)
    return jnp.concatenate([rot.astype(x.dtype), x[..., ROT_DIM:]], axis=-1)


def diff_attn_qkv(xn, w_in, qn_g, kn_g, pos):
    B, T, _ = xn.shape
    q, k, v = jnp.split(xn @ w_in, 3, axis=-1)
    q = partial_rope(rmsnorm(q.reshape(B, T, AT_HEADS, 2, AT_HEAD_DIM), qn_g), pos)
    k = partial_rope(rmsnorm(k.reshape(B, T, AT_HEADS, 2, AT_HEAD_DIM), kn_g), pos)
    v = v.reshape(B, T, AT_HEADS, 2 * AT_HEAD_DIM)
    return q, k, v


def diff_lambda(lq1, lk1, lq2, lk2, lam_init):
    f = jnp.float32
    return (jnp.exp(jnp.sum(lq1.astype(f) * lk1.astype(f)))
            - jnp.exp(jnp.sum(lq2.astype(f) * lk2.astype(f))) + lam_init)


def diff_core(q, k, v, mask, lam):
    s = jnp.einsum('bqhcd,bkhcd->bhcqk', q, k).astype(jnp.float32) * ATTN_SCALE
    p = jax.nn.softmax(jnp.where(mask, s, NEG_INF), axis=-1)
    w = p[:, :, 0] - lam * p[:, :, 1]
    return jnp.einsum('bhqk,bkhe->bqhe', w.astype(v.dtype), v)


def diff_attn_prompt(q, k, v, lam):
    B, T = q.shape[:2]
    kpos = jnp.arange(T)

    def block(i):
        start = i * Q_BLOCK
        qb = lax.dynamic_slice_in_dim(q, start, Q_BLOCK, axis=1)
        mask = kpos[None, :] <= (start + jnp.arange(Q_BLOCK))[:, None]
        return diff_core(qb, k, v, mask, lam)

    o = lax.map(block, jnp.arange(T // Q_BLOCK))
    return jnp.swapaxes(o, 0, 1).reshape(B, T, AT_HEADS, 2 * AT_HEAD_DIM)


def diff_attn_sample(q, k, v, cache_k, cache_v, layer, page_table, lam):
    S = q.shape[1]
    past = page_table.shape[1] * cache_k.shape[2]
    mask = jnp.concatenate([jnp.ones((S, past), bool), jnp.tril(jnp.ones((S, S), bool))], axis=1)

    def one(args):
        qb, kb, vb, pages = args
        kp = cache_k[layer, pages].reshape(past, AT_HEADS, 2, AT_HEAD_DIM).astype(kb.dtype)
        vp = cache_v[layer, pages].reshape(past, AT_HEADS, 2 * AT_HEAD_DIM).astype(vb.dtype)
        k_all = jnp.concatenate([kp, kb], axis=0)
        v_all = jnp.concatenate([vp, vb], axis=0)
        return diff_core(qb[None], k_all[None], v_all[None], mask, lam)[0]

    return lax.map(one, (q, k, v, page_table))


def diff_out(o, subln_g, lam_init, wo):
    B, T = o.shape[:2]
    return (rmsnorm(o, subln_g) * (1.0 - lam_init)).reshape(B, T, D_MODEL) @ wo


def swiglu(xn, w_gu, w_down):
    gt, up = jnp.split(xn @ w_gu, 2, axis=-1)
    return (jax.nn.silu(gt) * up) @ w_down


def moe_swiglu(xn, router, w_gu, w_down):
    logits = (xn @ router).astype(jnp.float32)
    top_v, top_i = lax.top_k(logits, TOP_K)
    wts = jax.nn.softmax(top_v, axis=-1)
    gates = jnp.sum(jax.nn.one_hot(top_i, N_EXPERTS, dtype=jnp.float32) * wts[..., None], axis=-2)
    y = jnp.zeros_like(xn)
    for e in range(N_EXPERTS):
        y = y + gates[..., e:e + 1].astype(xn.dtype) * swiglu(xn, w_gu[e], w_down[e])
    return y


def setup_inputs(seed: int = 0) -> dict:
    key = jax.random.key(seed)
    keys = iter(jax.random.split(key, 64))
    f32 = jnp.float32
    NR, NA, D = N_RWKV_LAYERS, N_ATTN_LAYERS, D_MODEL
    n_pages = PAST_LEN // PAGE_SIZE
    n_used = DEC_BATCH * n_pages
    n_pool = n_used + n_used // 4

    def nrm(shape, scale):
        return jax.random.normal(next(keys), shape, f32) * scale

    def gain(shape):
        return 1.0 + nrm(shape, 0.05)

    d = {}
    d["x_prompt"] = nrm((BATCH, SEQ, D), 1.0)
    d["x_sample"] = nrm((DEC_BATCH, DEC_SEQ, D), 1.0)
    d["state_wkv"] = nrm((NR, DEC_BATCH, RK_HEADS, RK_HEAD, RK_HEAD), 0.1)
    d["state_shift"] = nrm((NR, DEC_BATCH, D), 1.0)
    d["cache_k"] = nrm((NA, n_pool, PAGE_SIZE, AT_HEADS, 2, AT_HEAD_DIM), 1.0)
    d["cache_v"] = nrm((NA, n_pool, PAGE_SIZE, AT_HEADS, 2 * AT_HEAD_DIM), 1.0)
    d["page_table"] = jax.random.permutation(next(keys), n_pool)[:n_used].reshape(
        DEC_BATCH, n_pages).astype(jnp.int32)
    d["rk_norm_g"] = gain((NR, D))
    d["rk_mix"] = jax.random.uniform(next(keys), (NR, 6, D), f32, 0.0, 1.0)
    d["rk_wr"] = nrm((NR, D, D), D ** -0.5)
    d["rk_wk"] = nrm((NR, D, D), D ** -0.5)
    d["rk_wv"] = nrm((NR, D, D), D ** -0.5)
    d["rk_w0"] = jnp.linspace(-6.5, -1.5, D, dtype=f32)[None, :] + nrm((NR, D), 0.1)
    d["rk_w1"] = nrm((NR, D, DECAY_LORA), D ** -0.5)
    d["rk_w2"] = nrm((NR, DECAY_LORA, D), 0.5 * DECAY_LORA ** -0.5)
    d["rk_a0"] = nrm((NR, D), 0.1)
    d["rk_a1"] = nrm((NR, D, ICLR_LORA), D ** -0.5)
    d["rk_a2"] = nrm((NR, ICLR_LORA, D), 0.5 * ICLR_LORA ** -0.5)
    d["rk_g1"] = nrm((NR, D, GATE_LORA), D ** -0.5)
    d["rk_g2"] = nrm((NR, GATE_LORA, D), GATE_LORA ** -0.5)
    d["rk_k_k"] = 0.85 + nrm((NR, D), 0.05)
    d["rk_k_a"] = gain((NR, D))
    d["rk_r_k"] = nrm((NR, RK_HEADS, RK_HEAD), 0.1)
    d["rk_lnx_g"] = gain((NR, D))
    d["rk_lnx_b"] = nrm((NR, D), 0.02)
    d["rk_wo"] = nrm((NR, D, D), D ** -0.5)
    d["ffn_norm_g"] = gain((NR, D))
    d["ffn_w_gu"] = nrm((NR, D, 2 * D_FF), D ** -0.5)
    d["ffn_w_down"] = nrm((NR, D_FF, D), D_FF ** -0.5)
    d["at_norm_g"] = gain((NA, D))
    d["at_w_in"] = nrm((NA, D, 3 * D), D ** -0.5)
    d["at_q_norm_g"] = gain((NA, AT_HEAD_DIM))
    d["at_k_norm_g"] = gain((NA, AT_HEAD_DIM))
    d["at_lq1"] = nrm((NA, AT_HEAD_DIM), 0.1)
    d["at_lk1"] = nrm((NA, AT_HEAD_DIM), 0.1)
    d["at_lq2"] = nrm((NA, AT_HEAD_DIM), 0.1)
    d["at_lk2"] = nrm((NA, AT_HEAD_DIM), 0.1)
    d["at_subln_g"] = gain((NA, 2 * AT_HEAD_DIM))
    d["at_wo"] = nrm((NA, D, D), D ** -0.5)
    d["moe_norm_g"] = gain((NA, D))
    d["moe_router"] = nrm((NA, D, N_EXPERTS), D ** -0.5)
    d["moe_w_gu"] = nrm((NA, N_EXPERTS, D, 2 * D_FF_EXPERT), D ** -0.5)
    d["moe_w_down"] = nrm((NA, N_EXPERTS, D_FF_EXPERT, D), D_FF_EXPERT ** -0.5)
    return d


def reference(x_prompt, x_sample, state_wkv, state_shift, cache_k, cache_v, page_table,
              rk_norm_g, rk_mix, rk_wr, rk_wk, rk_wv, rk_w0, rk_w1, rk_w2, rk_a0, rk_a1, rk_a2,
              rk_g1, rk_g2, rk_k_k, rk_k_a, rk_r_k, rk_lnx_g, rk_lnx_b, rk_wo,
              ffn_norm_g, ffn_w_gu, ffn_w_down,
              at_norm_g, at_w_in, at_q_norm_g, at_k_norm_g, at_lq1, at_lk1, at_lq2, at_lk2,
              at_subln_g, at_wo,
              moe_norm_g, moe_router, moe_w_gu, moe_w_down):
    yp, ys = x_prompt, x_sample
    Bp, Bs = x_prompt.shape[0], x_sample.shape[0]
    past = page_table.shape[1] * cache_k.shape[2]
    pos_p = jnp.arange(x_prompt.shape[1])
    pos_s = past + jnp.arange(x_sample.shape[1])
    p_wkv, p_shift, p_k, p_v = [], [], [], []
    s_wkv, s_shift, s_k, s_v = [], [], [], []
    for i in range(DEPTH):
        j = i // N_MIXERS
        if i % N_MIXERS == 0:
            rk = (rk_mix[j], rk_wr[j], rk_wk[j], rk_wv[j], rk_w0[j], rk_w1[j], rk_w2[j],
                  rk_a0[j], rk_a1[j], rk_a2[j], rk_g1[j], rk_g2[j], rk_k_k[j], rk_k_a[j],
                  rk_r_k[j], rk_lnx_g[j], rk_lnx_b[j], rk_wo[j])
            xn = rmsnorm(yp, rk_norm_g[j])
            out, sh, st = rwkv7_time_mix(
                xn, jnp.zeros((Bp, D_MODEL), xn.dtype),
                jnp.zeros((Bp, RK_HEADS, RK_HEAD, RK_HEAD), xn.dtype), *rk)
            yp = yp + out
            p_shift.append(sh)
            p_wkv.append(st)
            xn = rmsnorm(ys, rk_norm_g[j])
            out, sh, st = rwkv7_time_mix(xn, state_shift[j], state_wkv[j], *rk)
            ys = ys + out
            s_shift.append(sh)
            s_wkv.append(st)
        else:
            lam_init = 0.8 - 0.6 * math.exp(-0.3 * i)
            lam = diff_lambda(at_lq1[j], at_lk1[j], at_lq2[j], at_lk2[j], lam_init)
            xn = rmsnorm(yp, at_norm_g[j])
            q, k, v = diff_attn_qkv(xn, at_w_in[j], at_q_norm_g[j], at_k_norm_g[j], pos_p)
            yp = yp + diff_out(diff_attn_prompt(q, k, v, lam), at_subln_g[j], lam_init, at_wo[j])
            p_k.append(k)
            p_v.append(v)
            xn = rmsnorm(ys, at_norm_g[j])
            q, k, v = diff_attn_qkv(xn, at_w_in[j], at_q_norm_g[j], at_k_norm_g[j], pos_s)
            o = diff_attn_sample(q, k, v, cache_k, cache_v, j, page_table, lam)
            ys = ys + diff_out(o, at_subln_g[j], lam_init, at_wo[j])
            s_k.append(k)
            s_v.append(v)
        if i % 2 == 0:
            yp = yp + swiglu(rmsnorm(yp, ffn_norm_g[j]), ffn_w_gu[j], ffn_w_down[j])
            ys = ys + swiglu(rmsnorm(ys, ffn_norm_g[j]), ffn_w_gu[j], ffn_w_down[j])
        else:
            yp = yp + moe_swiglu(rmsnorm(yp, moe_norm_g[j]), moe_router[j], moe_w_gu[j], moe_w_down[j])
            ys = ys + moe_swiglu(rmsnorm(ys, moe_norm_g[j]), moe_router[j], moe_w_gu[j], moe_w_down[j])
    return (yp, ys, jnp.stack(p_wkv), jnp.stack(p_shift), jnp.stack(p_k), jnp.stack(p_v),
            jnp.stack(s_wkv), jnp.stack(s_shift), jnp.stack(s_k), jnp.stack(s_v))
```

```python
import functools
import math

import jax
import jax.numpy as jnp
from jax import lax
from jax.experimental import pallas as pl
from jax.experimental.pallas import tpu as pltpu

F32 = jnp.float32
BF16 = jnp.bfloat16

RK_HEAD = 64
GN_EPS = 64e-5
AT_HEAD_DIM = 64
ROT_DIM = AT_HEAD_DIM // 4
ROPE_THETA = 500000.0
ATTN_SCALE = AT_HEAD_DIM ** -0.5
NEG_INF = -1e30
N_EXPERTS = 8
RMS_EPS = 1e-6
LANES = 128
VMEM_LIMIT = 56 * 1024 * 1024


def _cparams(sem):
    return pltpu.CompilerParams(dimension_semantics=sem, vmem_limit_bytes=VMEM_LIMIT)


def _dg(a, b, ca, cb):
    return lax.dot_general(a, b, (((ca,), (cb,)), ((), ())), preferred_element_type=F32)


def _split(x, n):
    parts = []
    for i in range(n):
        p = x.astype(BF16)
        parts.append(p)
        if i + 1 < n:
            x = x - p.astype(F32)
    return parts


def _dot(a, b, ca=1, cb=0, passes=3):
    if passes == 1:
        return _dg(a.astype(BF16), b.astype(BF16), ca, cb)
    if passes == 3:
        ah, al = _split(a, 2)
        bh, bl = _split(b, 2)
        return _dg(ah, bh, ca, cb) + (_dg(ah, bl, ca, cb) + _dg(al, bh, ca, cb))
    ah, am, al = _split(a, 3)
    bh, bm, bl = _split(b, 3)
    return (_dg(ah, bh, ca, cb) + (_dg(ah, bm, ca, cb) + _dg(am, bh, ca, cb))
            + (_dg(am, bm, ca, cb) + _dg(ah, bl, ca, cb) + _dg(al, bh, ca, cb)))


def _dot_exact_rhs(a, b_bf16, ca=1, cb=0):
    ah, am, al = _split(a, 3)
    return _dg(ah, b_bf16, ca, cb) + (_dg(am, b_bf16, ca, cb) + _dg(al, b_bf16, ca, cb))


def _dot_exact_lhs(a_bf16, b, ca=1, cb=0):
    bh, bm, bl = _split(b, 3)
    return _dg(a_bf16, bh, ca, cb) + (_dg(a_bf16, bm, ca, cb) + _dg(a_bf16, bl, ca, cb))


def _iota(shape, dim):
    return lax.broadcasted_iota(jnp.int32, shape, dim)


def _rmsnorm_kernel(x_ref, g_ref, o_ref):
    x = x_ref[...]
    o_ref[...] = x * lax.rsqrt(jnp.mean(x * x, axis=-1, keepdims=True) + RMS_EPS) * g_ref[...]


def _rmsnorm_rows(x, g, tm):
    m, d = x.shape
    return pl.pallas_call(
        _rmsnorm_kernel,
        grid=(m // tm,),
        in_specs=[pl.BlockSpec((tm, d), lambda i: (i, 0)), pl.BlockSpec((1, d), lambda i: (0, 0))],
        out_specs=pl.BlockSpec((tm, d), lambda i: (i, 0)),
        out_shape=jax.ShapeDtypeStruct((m, d), F32),
        compiler_params=_cparams(("parallel",)),
        name="rmsnorm",
    )(x, g.reshape(1, d))


def _rk_proj_kernel(xn_ref, prev_ref, mix_ref, wr_ref, wk_ref, wv_ref, w1_ref, w2_ref, w0_ref,
                    a1_ref, a2_ref, a0_ref, g1_ref, g2_ref,
                    r_ref, k_ref, v_ref, wl_ref, a_ref, g_ref):
    xn = xn_ref[...]
    dx = prev_ref[...] - xn
    mix = mix_ref[...]

    def mixed(c):
        return (xn + dx * mix[c:c + 1, :]).astype(BF16)

    def mm(x, w_ref):
        return jnp.dot(x, w_ref[...], preferred_element_type=F32)

    r_ref[...] = mm(mixed(0), wr_ref)
    k_ref[...] = mm(mixed(2), wk_ref)
    v_ref[...] = mm(mixed(3), wv_ref)
    z = w0_ref[...] + mm(jnp.tanh(mm(mixed(1), w1_ref)).astype(BF16), w2_ref)
    wl_ref[...] = -jax.nn.softplus(-z) - 0.5
    a_ref[...] = jax.nn.sigmoid(a0_ref[...] + mm(mm(mixed(4), a1_ref).astype(BF16), a2_ref))
    g_ref[...] = mm(jax.nn.sigmoid(mm(mixed(5), g1_ref)).astype(BF16), g2_ref)


def _pad_to(x, axis, mult):
    pad = (-x.shape[axis]) % mult
    if pad == 0:
        return x
    widths = [(0, 0)] * x.ndim
    widths[axis] = (0, pad)
    return jnp.pad(x, widths)


def _rk_proj(xn, prev, p, tm):
    m, d = xn.shape
    bf = lambda w: w.astype(BF16)
    w1, w2 = bf(_pad_to(p["w1"], 1, LANES)), bf(_pad_to(p["w2"], 0, LANES))
    a1, a2 = bf(_pad_to(p["a1"], 1, LANES)), bf(_pad_to(p["a2"], 0, LANES))
    g1, g2 = bf(_pad_to(p["g1"], 1, LANES)), bf(_pad_to(p["g2"], 0, LANES))
    row = pl.BlockSpec((tm, d), lambda i: (i, 0))
    full = lambda a: pl.BlockSpec(a.shape, lambda i: (0,) * a.ndim)
    args = (xn, prev, p["mix"], bf(p["wr"]), bf(p["wk"]), bf(p["wv"]), w1, w2, p["w0"].reshape(1, d),
            a1, a2, p["a0"].reshape(1, d), g1, g2)
    return pl.pallas_call(
        _rk_proj_kernel,
        grid=(m // tm,),
        in_specs=[row, row] + [full(a) for a in args[2:]],
        out_specs=[row] * 6,
        out_shape=[jax.ShapeDtypeStruct((m, d), F32)] * 6,
        compiler_params=_cparams(("parallel",)),
        name="rk_proj",
    )(*args)


def _rk_scan_kernel(r_ref, k_ref, v_ref, wl_ref, a_ref, g_ref, kk_ref, ka_ref, rk_ref, lg_ref, lb_ref,
                    s0_ref, z_ref, sT_ref, s_scr, *, chunk, t_valid, passes):
    c = pl.program_id(2)
    C = chunk
    C2 = 2 * C

    @pl.when(c == 0)
    def _():
        s_scr[...] = s0_ref[0, 0]

    r = r_ref[0]
    k = k_ref[0]
    v = v_ref[0]
    a = a_ref[0]
    lane = _iota((C, LANES), 1)
    head0 = lane < RK_HEAD
    same_head = ((_iota((LANES, LANES), 0) < RK_HEAD) == (_iota((LANES, LANES), 1) < RK_HEAD))
    bd_ones = same_head.astype(BF16)

    logw = -jnp.exp(wl_ref[0])
    kk = k * kk_ref[...]
    ss = _dot_exact_rhs(kk * kk, bd_ones)
    kk = kk / jnp.maximum(jnp.sqrt(ss), 1e-12)
    k2 = k * (1.0 + (a - 1.0) * ka_ref[...])
    b = kk * a
    if t_valid < C:
        valid = _iota((C, LANES), 0) < t_valid
        logw = jnp.where(valid, logw, 0.0)
        b = jnp.where(valid, b, 0.0)
        k2m = jnp.where(valid, k2, 0.0)
    else:
        k2m = k2

    tri_incl = (_iota((C, C), 0) >= _iota((C, C), 1)).astype(BF16)
    cum = _dot_exact_lhs(tri_incl, logw)
    gam = jnp.exp(cum)
    inv_gam = jnp.exp(-cum)
    gam_prev = jnp.exp(cum - logw)
    g_last = gam[C - 1:C, :]

    at = -kk * gam_prev
    bt = b * inv_gam
    kt = k2m * inv_gam
    rt = r * gam

    def stack(x):
        return jnp.concatenate([jnp.where(head0, x, 0.0), jnp.where(head0, 0.0, x)], axis=0)

    def fold(x):
        return x[:C] + x[C:]

    ats, bts, kts, rts, vs = stack(at), stack(bt), stack(kt), stack(rt), stack(v)
    ri = _iota((C2, C2), 0)
    ci = _iota((C2, C2), 1)
    same_blk = (ri < C) == (ci < C)
    strict = same_blk & (ri > ci)
    incl = same_blk & (ri >= ci)
    dp = functools.partial(_dot, passes=passes)
    a_ab = jnp.where(strict, dp(ats, bts, 1, 1), 0.0)
    a_ak = jnp.where(strict, dp(ats, kts, 1, 1), 0.0)
    m_rb = jnp.where(incl, dp(rts, bts, 1, 1), 0.0)
    m_rk = jnp.where(incl, dp(rts, kts, 1, 1), 0.0)

    x = a_ab
    t_inv = jnp.where(ri == ci, 1.0, 0.0) + x
    for _ in range(int(math.log2(C)) - 1):
        x = dp(x, x)
        t_inv = t_inv + dp(t_inv, x)

    ws = dp(a_ak, vs)
    p1s = dp(t_inv, ats)
    p2s = dp(t_inv, ws)
    q1 = fold(rts + dp(m_rb, p1s))
    q2 = fold(dp(m_rb, p2s) + dp(m_rk, vs))
    p1 = fold(p1s)
    p2 = fold(p2s)

    s = s_scr[...]
    y = dp(q1, s, 1, 1) + q2
    bp = bt * g_last
    kp = kt * g_last
    eye = _iota((LANES, LANES), 0) == _iota((LANES, LANES), 1)
    gm = jnp.where(eye, g_last, 0.0) + jnp.where(same_head, dp(p1, bp, 0, 0), 0.0)
    hm = jnp.where(same_head, dp(p2, bp, 0, 0) + dp(v, kp, 0, 0), 0.0)
    s_new = dp(s, gm) + hm
    s_scr[...] = s_new

    @pl.when(c == pl.num_programs(2) - 1)
    def _():
        sT_ref[0, 0] = s_new

    inv_n = 1.0 / RK_HEAD
    mu = _dot_exact_rhs(y, bd_ones) * inv_n
    d = y - mu
    var = _dot_exact_rhs(d * d, bd_ones) * inv_n
    yn = d * lax.rsqrt(var + GN_EPS) * lg_ref[...] + lb_ref[...]
    bonus = _dot_exact_rhs(r * k2 * rk_ref[...], bd_ones) * v
    z_ref[0] = (yn + bonus) * g_ref[0]


def _pair_state(s):
    bsz, h, n, _ = s.shape
    s = s.reshape(bsz, h // 2, 2, n, n)
    z = jnp.zeros_like(s[:, :, 0])
    top = jnp.concatenate([s[:, :, 0], z], axis=-1)
    bot = jnp.concatenate([z, s[:, :, 1]], axis=-1)
    return jnp.concatenate([top, bot], axis=-2)


def _unpair_state(sp):
    n = sp.shape[-1] // 2
    s = jnp.stack([sp[:, :, :n, :n], sp[:, :, n:, n:]], axis=2)
    return s.reshape(sp.shape[0], sp.shape[1] * 2, n, n)


def _rk_scan(r, k, v, wl, a, g, p, s0, chunk, t_valid, passes=3):
    bsz, t, d = r.shape
    npair = d // LANES
    nchunk = t // chunk
    tok = pl.BlockSpec((1, chunk, LANES), lambda b, h, c: (b, c, h))
    par = pl.BlockSpec((1, LANES), lambda b, h, c: (0, h))
    st = pl.BlockSpec((1, 1, LANES, LANES), lambda b, h, c: (b, h, 0, 0))
    vec = lambda x: x.reshape(1, d)
    z, s_t = pl.pallas_call(
        functools.partial(_rk_scan_kernel, chunk=chunk, t_valid=t_valid, passes=passes),
        grid=(bsz, npair, nchunk),
        in_specs=[tok] * 6 + [par] * 5 + [st],
        out_specs=[tok, st],
        out_shape=[jax.ShapeDtypeStruct((bsz, t, d), F32),
                   jax.ShapeDtypeStruct((bsz, npair, LANES, LANES), F32)],
        scratch_shapes=[pltpu.VMEM((LANES, LANES), F32)],
        compiler_params=_cparams(("parallel", "parallel", "arbitrary")),
        name="rk_scan",
    )(r, k, v, wl, a, g, vec(p["k_k"]), vec(p["k_a"]), vec(p["r_k"]), vec(p["lnx_g"]), vec(p["lnx_b"]),
      _pair_state(s0))
    return z, _unpair_state(s_t)


def _mm_res_kernel(x_ref, w_ref, res_ref, o_ref):
    o_ref[...] = res_ref[...] + jnp.dot(x_ref[...].astype(BF16), w_ref[...], preferred_element_type=F32)


def _mm_res(x, w, res, tm):
    m, kd = x.shape
    n = w.shape[1]
    return pl.pallas_call(
        _mm_res_kernel,
        grid=(m // tm,),
        in_specs=[pl.BlockSpec((tm, kd), lambda i: (i, 0)), pl.BlockSpec((kd, n), lambda i: (0, 0)),
                  pl.BlockSpec((tm, n), lambda i: (i, 0))],
        out_specs=pl.BlockSpec((tm, n), lambda i: (i, 0)),
        out_shape=jax.ShapeDtypeStruct((m, n), F32),
        compiler_params=_cparams(("parallel",)),
        name="mm_res",
    )(x, w.astype(BF16), res)


def _rms(x, g):
    return x * lax.rsqrt(jnp.mean(x * x, axis=-1, keepdims=True) + RMS_EPS) * g


def _ffn_kernel(x_ref, g_ref, wg_ref, wu_ref, wd_ref, o_ref):
    x = x_ref[...]
    xn = _rms(x, g_ref[...]).astype(BF16)
    gt = jnp.dot(xn, wg_ref[...], preferred_element_type=F32)
    up = jnp.dot(xn, wu_ref[...], preferred_element_type=F32)
    h = (jax.nn.silu(gt) * up).astype(BF16)
    o_ref[...] = x + jnp.dot(h, wd_ref[...], preferred_element_type=F32)


def _ffn(x, g, w_gu, w_down, tm):
    m, d = x.shape
    ff = w_down.shape[0]
    wg, wu = w_gu[:, :ff].astype(BF16), w_gu[:, ff:].astype(BF16)
    full = lambda a: pl.BlockSpec(a.shape, lambda i: (0,) * a.ndim)
    row = pl.BlockSpec((tm, d), lambda i: (i, 0))
    wd = w_down.astype(BF16)
    return pl.pallas_call(
        _ffn_kernel,
        grid=(m // tm,),
        in_specs=[row, pl.BlockSpec((1, d), lambda i: (0, 0)), full(wg), full(wu), full(wd)],
        out_specs=row,
        out_shape=jax.ShapeDtypeStruct((m, d), F32),
        compiler_params=_cparams(("parallel",)),
        name="ffn",
    )(x, g.reshape(1, d), wg, wu, wd)


def _moe_kernel(x_ref, g_ref, rt_ref, wg_ref, wu_ref, wd_ref, o_ref, xn_scr, gate_scr):
    e = pl.program_id(1)

    @pl.when(e == 0)
    def _():
        x = x_ref[...]
        xn = _rms(x, g_ref[...])
        xn_scr[...] = xn.astype(BF16)
        logits = _dot(xn, rt_ref[...], passes=6)
        lane = _iota(logits.shape, 1)
        logits = jnp.where(lane < N_EXPERTS, logits, NEG_INF)
        v1 = jnp.max(logits, axis=-1, keepdims=True)
        i1 = jnp.min(jnp.where(logits == v1, lane, LANES), axis=-1, keepdims=True)
        rest = jnp.where(lane == i1, NEG_INF, logits)
        v2 = jnp.max(rest, axis=-1, keepdims=True)
        i2 = jnp.min(jnp.where(rest == v2, lane, LANES), axis=-1, keepdims=True)
        e2 = jnp.exp(v2 - v1)
        w1 = 1.0 / (1.0 + e2)
        w2 = e2 / (1.0 + e2)
        gate_scr[...] = jnp.where(lane == i1, w1, 0.0) + jnp.where(lane == i2, w2, 0.0)
        o_ref[...] = x

    xn = xn_scr[...]
    gt = jnp.dot(xn, wg_ref[0], preferred_element_type=F32)
    up = jnp.dot(xn, wu_ref[0], preferred_element_type=F32)
    h = (jax.nn.silu(gt) * up).astype(BF16)
    y = jnp.dot(h, wd_ref[0], preferred_element_type=F32)
    gates = gate_scr[...]
    gate_e = jnp.sum(jnp.where(_iota(gates.shape, 1) == e, gates, 0.0), axis=-1, keepdims=True)
    o_ref[...] += gate_e * y


def _moe(x, g, router, w_gu, w_down, tm):
    m, d = x.shape
    ne, ffe = w_down.shape[0], w_down.shape[1]
    wg, wu = w_gu[:, :, :ffe].astype(BF16), w_gu[:, :, ffe:].astype(BF16)
    wd = w_down.astype(BF16)
    rt = _pad_to(router, 1, LANES)
    row = pl.BlockSpec((tm, d), lambda i, e: (i, 0))
    return pl.pallas_call(
        _moe_kernel,
        grid=(m // tm, ne),
        in_specs=[row, pl.BlockSpec((1, d), lambda i, e: (0, 0)), pl.BlockSpec(rt.shape, lambda i, e: (0, 0)),
                  pl.BlockSpec((1, d, ffe), lambda i, e: (e, 0, 0)),
                  pl.BlockSpec((1, d, ffe), lambda i, e: (e, 0, 0)),
                  pl.BlockSpec((1, ffe, d), lambda i, e: (e, 0, 0))],
        out_specs=row,
        out_shape=jax.ShapeDtypeStruct((m, d), F32),
        scratch_shapes=[pltpu.VMEM((tm, d), BF16), pltpu.VMEM((tm, LANES), F32)],
        compiler_params=_cparams(("parallel", "arbitrary")),
        name="moe",
    )(x, g.reshape(1, d), rt, wg, wu, wd)


def _qkv_kernel(x_ref, g_ref, w_ref, o_ref):
    xn = _rms(x_ref[...], g_ref[...]).astype(BF16)
    o_ref[...] = jnp.dot(xn, w_ref[...], preferred_element_type=F32)


def _qkv_proj(x, g, w_in, tm, tn):
    m, d = x.shape
    n = w_in.shape[1]
    return pl.pallas_call(
        _qkv_kernel,
        grid=(m // tm, n // tn),
        in_specs=[pl.BlockSpec((tm, d), lambda i, j: (i, 0)), pl.BlockSpec((1, d), lambda i, j: (0, 0)),
                  pl.BlockSpec((d, tn), lambda i, j: (0, j))],
        out_specs=pl.BlockSpec((tm, tn), lambda i, j: (i, j)),
        out_shape=jax.ShapeDtypeStruct((m, n), F32),
        compiler_params=_cparams(("parallel", "parallel")),
        name="qkv_proj",
    )(x, g.reshape(1, d), w_in.astype(BF16))


def _qk_post_kernel(x_ref, g_ref, cos_ref, sin_ref, o_ref):
    x = x_ref[...]
    lane = _iota(x.shape, 1)
    first = lane < AT_HEAD_DIM
    sq = x * x
    s0 = jnp.sum(jnp.where(first, sq, 0.0), axis=-1, keepdims=True)
    s1 = jnp.sum(jnp.where(first, 0.0, sq), axis=-1, keepdims=True)
    ms = jnp.where(first, s0, s1) * (1.0 / AT_HEAD_DIM)
    y = x * lax.rsqrt(ms + RMS_EPS) * g_ref[...]
    half = ROT_DIM // 2
    in_lo = (lane % AT_HEAD_DIM) < half
    partner = jnp.where(in_lo, pltpu.roll(y, LANES - half, axis=1), pltpu.roll(y, half, axis=1))
    o_ref[...] = y * cos_ref[...] + partner * sin_ref[...]


def _rope_tables(pos):
    half = ROT_DIM // 2
    inv = ROPE_THETA ** (-jnp.arange(0, ROT_DIM, 2, dtype=F32) / ROT_DIM)
    ang = pos.astype(F32)[:, None] * inv[None, :]
    cos, sin = jnp.cos(ang), jnp.sin(ang)
    ones = jnp.ones((pos.shape[0], AT_HEAD_DIM - ROT_DIM), F32)
    cos_g = jnp.concatenate([cos, cos, ones], axis=1)
    sin_g = jnp.concatenate([-sin, sin, 0.0 * ones], axis=1)
    return jnp.tile(cos_g, (1, 2)), jnp.tile(sin_g, (1, 2))


def _qk_post(qkv, qn_g, kn_g, cos_t, sin_t, tm):
    m = qkv.shape[0]
    d = qkv.shape[1] // 3
    nh = d // LANES
    gains = jnp.stack([jnp.tile(qn_g, 2), jnp.tile(kn_g, 2)]).reshape(2, 1, LANES)
    nt = cos_t.shape[0] // tm
    return pl.pallas_call(
        _qk_post_kernel,
        grid=(m // tm, 2 * nh),
        in_specs=[pl.BlockSpec((tm, LANES), lambda i, j: (i, j)),
                  pl.BlockSpec((None, 1, LANES), lambda i, j: (j // nh, 0, 0)),
                  pl.BlockSpec((tm, LANES), lambda i, j: (i % nt, 0)),
                  pl.BlockSpec((tm, LANES), lambda i, j: (i % nt, 0))],
        out_specs=pl.BlockSpec((tm, LANES), lambda i, j: (i, j)),
        out_shape=jax.ShapeDtypeStruct((m, 2 * d), F32),
        compiler_params=_cparams(("parallel", "parallel")),
        name="qk_post",
    )(qkv, gains, cos_t, sin_t)


def _diff_finish(o1, o2, lam, sub_g, lam_init):
    o = o1 - lam * o2
    o = o * lax.rsqrt(jnp.mean(o * o, axis=-1, keepdims=True) + RMS_EPS) * sub_g
    return o * (1.0 - lam_init)


def _softmax_block(s, v_bf, m_ref, l_ref, acc_ref):
    m_prev = m_ref[...]
    m_new = jnp.maximum(m_prev, jnp.max(s, axis=-1, keepdims=True))
    alpha = jnp.exp(m_prev - m_new)
    p = jnp.exp(s - m_new)
    l_ref[...] = alpha * l_ref[...] + jnp.sum(p, axis=-1, keepdims=True)
    acc_ref[...] = alpha * acc_ref[...] + jnp.dot(p.astype(BF16), v_bf, preferred_element_type=F32)
    m_ref[...] = m_new


def _attn_prompt_kernel(lam_ref, q_ref, k_ref, v_ref, sg_ref, o_ref, m_ref, l_ref, acc_ref,
                        *, tq, tk, lam_init):
    qi = pl.program_id(2)
    kj = pl.program_id(3)

    @pl.when(kj == 0)
    def _():
        m_ref[...] = jnp.full(m_ref.shape, NEG_INF, F32)
        l_ref[...] = jnp.zeros(l_ref.shape, F32)
        acc_ref[...] = jnp.zeros(acc_ref.shape, F32)

    @pl.when(kj * tk <= qi * tq + (tq - 1))
    def _():
        q = q_ref[0] * ATTN_SCALE
        first = _iota(q.shape, 1) < AT_HEAD_DIM
        qs = jnp.concatenate([jnp.where(first, q, 0.0), jnp.where(first, 0.0, q)], axis=0).astype(BF16)
        s = _dg(qs, k_ref[0].astype(BF16), 1, 1)
        qpos = qi * tq + (_iota(s.shape, 0) % tq)
        kpos = kj * tk + _iota(s.shape, 1)
        s = jnp.where(kpos <= qpos, s, NEG_INF)
        _softmax_block(s, v_ref[0].astype(BF16), m_ref, l_ref, acc_ref)

    @pl.when(kj == pl.num_programs(3) - 1)
    def _():
        o = acc_ref[...] / l_ref[...]
        o_ref[0] = _diff_finish(o[:tq], o[tq:], lam_ref[0], sg_ref[...], lam_init)


def _attn_prompt(qk, qkv, lam, sub_g, lam_init, bsz, seq, tq, tk):
    d = qkv.shape[1] // 3
    nh = d // LANES
    qk3 = qk.reshape(bsz, seq, 2 * d)
    qkv3 = qkv.reshape(bsz, seq, 3 * d)
    last_k = lambda i, j: jnp.minimum(j, (i * tq + tq - 1) // tk)
    out = pl.pallas_call(
        functools.partial(_attn_prompt_kernel, tq=tq, tk=tk, lam_init=lam_init),
        grid=(bsz, nh, seq // tq, seq // tk),
        in_specs=[pl.BlockSpec(memory_space=pltpu.SMEM),
                  pl.BlockSpec((1, tq, LANES), lambda b, h, i, j: (b, i, h)),
                  pl.BlockSpec((1, tk, LANES), lambda b, h, i, j: (b, last_k(i, j), nh + h)),
                  pl.BlockSpec((1, tk, LANES), lambda b, h, i, j: (b, last_k(i, j), 2 * nh + h)),
                  pl.BlockSpec((1, LANES), lambda b, h, i, j: (0, 0))],
        out_specs=pl.BlockSpec((1, tq, LANES), lambda b, h, i, j: (b, i, h)),
        out_shape=jax.ShapeDtypeStruct((bsz, seq, d), F32),
        scratch_shapes=[pltpu.VMEM((2 * tq, 1), F32), pltpu.VMEM((2 * tq, 1), F32),
                        pltpu.VMEM((2 * tq, LANES), F32)],
        compiler_params=_cparams(("parallel", "parallel", "parallel", "arbitrary")),
        name="attn_prompt",
    )(lam.reshape(1), qk3, qk3, qkv3, sub_g.reshape(1, LANES))
    return out.reshape(bsz * seq, d)


def _attn_decode_kernel(pt_ref, lam_ref, q_ref, kn_ref, vn_ref, ck_ref, cv_ref, sg_ref, o_ref,
                        m_ref, l_ref, acc_ref, *, nq, nh, lam_init):
    pg = pl.program_id(1)
    rows = q_ref.shape[1]
    half = rows // 2

    @pl.when(pg == 0)
    def _():
        m_ref[...] = jnp.full(m_ref.shape, NEG_INF, F32)
        l_ref[...] = jnp.zeros(l_ref.shape, F32)
        acc_ref[...] = jnp.zeros(acc_ref.shape, F32)

    def branch_q(h):
        q = q_ref[0, :, h * LANES:(h + 1) * LANES] * ATTN_SCALE
        keep = (_iota(q.shape, 0) < half) == (_iota(q.shape, 1) < AT_HEAD_DIM)
        return jnp.where(keep, q, 0.0)

    for h in range(nh):
        s = _dg(branch_q(h).astype(BF16), ck_ref[0, 0, :, h * LANES:(h + 1) * LANES].astype(BF16), 1, 1)
        _softmax_block(s, cv_ref[0, 0, :, h * LANES:(h + 1) * LANES].astype(BF16),
                       m_ref.at[h], l_ref.at[h], acc_ref.at[h])

    @pl.when(pg == pl.num_programs(1) - 1)
    def _():
        nk = kn_ref.shape[1]
        causal = _iota((rows, nk), 1) <= (_iota((rows, nk), 0) % half)
        causal = causal & (_iota((rows, nk), 1) < nq)
        for h in range(nh):
            lanes = slice(h * LANES, (h + 1) * LANES)
            s = jnp.where(causal, _dg(branch_q(h).astype(BF16), kn_ref[0, :, lanes].astype(BF16), 1, 1), NEG_INF)
            _softmax_block(s, vn_ref[0, :, lanes].astype(BF16), m_ref.at[h], l_ref.at[h], acc_ref.at[h])
            o = acc_ref[h] / l_ref[h]
            o_ref[0, :, lanes] = _diff_finish(o, pltpu.roll(o, half, axis=0), lam_ref[0], sg_ref[...],
                                              lam_init)


def _attn_decode(qk, qkv, cache_k, cache_v, layer, page_table, lam, sub_g, lam_init, bsz, nq):
    d = qkv.shape[1] // 3
    nh = d // LANES
    n_pages = page_table.shape[1]
    page = cache_k.shape[2]
    ck = cache_k.reshape(cache_k.shape[0], cache_k.shape[1], page, d)
    cv = cache_v.reshape(cache_v.shape[0], cache_v.shape[1], page, d)
    half = -(-nq // 8) * 8
    pad_rows = lambda x: _pad_to(x.reshape(bsz, nq, d), 1, half)
    q3 = pad_rows(qk[:, :d])
    q3 = jnp.concatenate([q3, q3], axis=1)
    k3 = _pad_to(pad_rows(qk[:, d:]), 1, 16)
    v3 = _pad_to(pad_rows(qkv[:, 2 * d:]), 1, 16)
    rows = 2 * half
    qspec = pl.BlockSpec((1, rows, d), lambda b, p, pt: (b, 0, 0))
    nspec = pl.BlockSpec((1, k3.shape[1], d), lambda b, p, pt: (b, 0, 0))
    grid_spec = pltpu.PrefetchScalarGridSpec(
        num_scalar_prefetch=1,
        grid=(bsz, n_pages),
        in_specs=[pl.BlockSpec(memory_space=pltpu.SMEM), qspec, nspec, nspec,
                  pl.BlockSpec((1, 1, page, d), lambda b, p, pt: (layer, pt[b, p], 0, 0)),
                  pl.BlockSpec((1, 1, page, d), lambda b, p, pt: (layer, pt[b, p], 0, 0)),
                  pl.BlockSpec((1, LANES), lambda b, p, pt: (0, 0))],
        out_specs=qspec,
        scratch_shapes=[pltpu.VMEM((nh, rows, 1), F32), pltpu.VMEM((nh, rows, 1), F32),
                        pltpu.VMEM((nh, rows, LANES), F32)],
    )
    out = pl.pallas_call(
        functools.partial(_attn_decode_kernel, nq=nq, nh=nh, lam_init=lam_init),
        grid_spec=grid_spec,
        out_shape=jax.ShapeDtypeStruct((bsz, rows, d), F32),
        compiler_params=_cparams(("parallel", "arbitrary")),
        name="attn_decode",
    )(page_table, lam.reshape(1), q3, k3, v3, ck, cv, sub_g.reshape(1, LANES))
    return out[:, :nq].reshape(bsz * nq, d)


def _row_tile(m, want):
    t = min(m, want)
    while m % t:
        t //= 2
    return t


def _rwkv_layer(x, shift_prev, s0, norm_g, p):
    bsz, t, d = x.shape
    m = bsz * t
    xn = _rmsnorm_rows(x.reshape(m, d), norm_g, _row_tile(m, 512)).reshape(bsz, t, d)
    prev = jnp.concatenate([shift_prev[:, None, :], xn[:, :-1]], axis=1)
    r, k, v, wl, a, g = _rk_proj(xn.reshape(m, d), prev.reshape(m, d), p, _row_tile(m, 256))
    chunk = 64 if t % 64 == 0 else 8
    t_pad = -(-t // chunk) * chunk
    seqs = [u.reshape(bsz, t, d) for u in (r, k, v, wl, a, g)]
    if t_pad != t:
        seqs = [_pad_to(u, 1, chunk) for u in seqs]
    z, s_t = _rk_scan(*seqs, p, s0, chunk, t if t_pad != t else chunk)
    z = z[:, :t].reshape(m, d)
    out = _mm_res(z, p["wo"], x.reshape(m, d), _row_tile(m, 512))
    return out.reshape(bsz, t, d), xn[:, -1], s_t


def _attn_layer(x, pos, norm_g, w_in, qn_g, kn_g, lam, lam_init, sub_g, wo, decode=None):
    bsz, t, d = x.shape
    m = bsz * t
    x2 = x.reshape(m, d)
    qkv = _qkv_proj(x2, norm_g, w_in, _row_tile(m, 512), 1024)
    cos_t, sin_t = _rope_tables(pos)
    if t % 8:
        cos_t, sin_t = jnp.tile(cos_t, (bsz, 1)), jnp.tile(sin_t, (bsz, 1))
    qk = _qk_post(qkv, qn_g, kn_g, cos_t, sin_t, _row_tile(cos_t.shape[0], 512))
    if decode is None:
        o = _attn_prompt(qk, qkv, lam, sub_g, lam_init, bsz, t, _row_tile(t, 512), _row_tile(t, 512))
    else:
        cache_k, cache_v, layer, page_table = decode
        o = _attn_decode(qk, qkv, cache_k, cache_v, layer, page_table, lam, sub_g, lam_init, bsz, t)
    y = _mm_res(o, wo, x2, _row_tile(m, 512)).reshape(bsz, t, d)
    nh = d // LANES
    k_out = qk[:, d:].reshape(bsz, t, nh, 2, AT_HEAD_DIM)
    v_out = qkv[:, 2 * d:].reshape(bsz, t, nh, 2 * AT_HEAD_DIM)
    return y, k_out, v_out


def kernel(x_prompt, x_sample, state_wkv, state_shift, cache_k, cache_v, page_table,
           rk_norm_g, rk_mix, rk_wr, rk_wk, rk_wv, rk_w0, rk_w1, rk_w2, rk_a0, rk_a1, rk_a2,
           rk_g1, rk_g2, rk_k_k, rk_k_a, rk_r_k, rk_lnx_g, rk_lnx_b, rk_wo,
           ffn_norm_g, ffn_w_gu, ffn_w_down,
           at_norm_g, at_w_in, at_q_norm_g, at_k_norm_g, at_lq1, at_lk1, at_lq2, at_lk2,
           at_subln_g, at_wo,
           moe_norm_g, moe_router, moe_w_gu, moe_w_down):
    yp, ys = x_prompt, x_sample
    bp, tp, d = x_prompt.shape
    bs, ts, _ = x_sample.shape
    depth = rk_norm_g.shape[0] + at_norm_g.shape[0]
    past = page_table.shape[1] * cache_k.shape[2]
    pos_p = jnp.arange(tp)
    pos_s = past + jnp.arange(ts)
    nheads = d // RK_HEAD
    p_wkv, p_shift, p_k, p_v = [], [], [], []
    s_wkv, s_shift, s_k, s_v = [], [], [], []
    for i in range(depth):
        j = i // 2
        if i % 2 == 0:
            p = dict(mix=rk_mix[j], wr=rk_wr[j], wk=rk_wk[j], wv=rk_wv[j], w0=rk_w0[j], w1=rk_w1[j],
                     w2=rk_w2[j], a0=rk_a0[j], a1=rk_a1[j], a2=rk_a2[j], g1=rk_g1[j], g2=rk_g2[j],
                     k_k=rk_k_k[j], k_a=rk_k_a[j], r_k=rk_r_k[j].reshape(-1), lnx_g=rk_lnx_g[j],
                     lnx_b=rk_lnx_b[j], wo=rk_wo[j])
            yp, sh, st = _rwkv_layer(yp, jnp.zeros((bp, d), F32),
                                     jnp.zeros((bp, nheads, RK_HEAD, RK_HEAD), F32), rk_norm_g[j], p)
            p_shift.append(sh)
            p_wkv.append(st)
            ys, sh, st = _rwkv_layer(ys, state_shift[j], state_wkv[j], rk_norm_g[j], p)
            s_shift.append(sh)
            s_wkv.append(st)
            yp = _ffn(yp.reshape(bp * tp, d), ffn_norm_g[j], ffn_w_gu[j], ffn_w_down[j],
                      _row_tile(bp * tp, 512)).reshape(bp, tp, d)
            ys = _ffn(ys.reshape(bs * ts, d), ffn_norm_g[j], ffn_w_gu[j], ffn_w_down[j],
                      _row_tile(bs * ts, 512)).reshape(bs, ts, d)
        else:
            lam_init = 0.8 - 0.6 * math.exp(-0.3 * i)
            lam = (jnp.exp(jnp.sum(at_lq1[j] * at_lk1[j])) - jnp.exp(jnp.sum(at_lq2[j] * at_lk2[j]))
                   + lam_init)
            common = (at_norm_g[j], at_w_in[j], at_q_norm_g[j], at_k_norm_g[j], lam, lam_init,
                      at_subln_g[j], at_wo[j])
            yp, k_new, v_new = _attn_layer(yp, pos_p, *common)
            p_k.append(k_new)
            p_v.append(v_new)
            ys, k_new, v_new = _attn_layer(ys, pos_s, *common, decode=(cache_k, cache_v, j, page_table))
            s_k.append(k_new)
            s_v.append(v_new)
            yp = _moe(yp.reshape(bp * tp, d), moe_norm_g[j], moe_router[j], moe_w_gu[j], moe_w_down[j],
                      _row_tile(bp * tp, 1024)).reshape(bp, tp, d)
            ys = _moe(ys.reshape(bs * ts, d), moe_norm_g[j], moe_router[j], moe_w_gu[j], moe_w_down[j],
                      _row_tile(bs * ts, 1024)).reshape(bs, ts, d)
    return (yp, ys, jnp.stack(p_wkv), jnp.stack(p_shift), jnp.stack(p_k), jnp.stack(p_v),
            jnp.stack(s_wkv), jnp.stack(s_shift), jnp.stack(s_k), jnp.stack(s_v))
```

```python
import functools
import math

import jax
import jax.numpy as jnp
from jax import lax
from jax.experimental import pallas as pl
from jax.experimental.pallas import tpu as pltpu

F32 = jnp.float32
BF16 = jnp.bfloat16

RK_HEAD = 64
GN_EPS = 64e-5
AT_HEAD_DIM = 64
ROT_DIM = AT_HEAD_DIM // 4
ROPE_THETA = 500000.0
ATTN_SCALE = AT_HEAD_DIM ** -0.5
LOG2E = 1.4426950408889634
NEG_INF = -1e30
N_EXPERTS = 8
RMS_EPS = 1e-6
LANES = 128
VMEM_LIMIT = 56 * 1024 * 1024


def _cparams(sem):
    return pltpu.CompilerParams(dimension_semantics=sem, vmem_limit_bytes=VMEM_LIMIT)


def _dg(a, b, ca, cb):
    return lax.dot_general(a, b, (((ca,), (cb,)), ((), ())), preferred_element_type=F32)


def _split(x, n):
    parts = []
    for i in range(n):
        p = x.astype(BF16)
        parts.append(p)
        if i + 1 < n:
            x = x - p.astype(F32)
    return parts


def _dot(a, b, ca=1, cb=0, passes=3):
    if passes == 1:
        return _dg(a.astype(BF16), b.astype(BF16), ca, cb)
    if passes == 3:
        ah, al = _split(a, 2)
        bh, bl = _split(b, 2)
        return _dg(ah, bh, ca, cb) + (_dg(ah, bl, ca, cb) + _dg(al, bh, ca, cb))
    ah, am, al = _split(a, 3)
    bh, bm, bl = _split(b, 3)
    return (_dg(ah, bh, ca, cb) + (_dg(ah, bm, ca, cb) + _dg(am, bh, ca, cb))
            + (_dg(am, bm, ca, cb) + _dg(ah, bl, ca, cb) + _dg(al, bh, ca, cb)))


def _dot_exact_rhs(a, b_bf16, ca=1, cb=0):
    ah, am, al = _split(a, 3)
    return _dg(ah, b_bf16, ca, cb) + (_dg(am, b_bf16, ca, cb) + _dg(al, b_bf16, ca, cb))


def _dot_exact_lhs(a_bf16, b, ca=1, cb=0):
    bh, bm, bl = _split(b, 3)
    return _dg(a_bf16, bh, ca, cb) + (_dg(a_bf16, bm, ca, cb) + _dg(a_bf16, bl, ca, cb))


def _iota(shape, dim):
    return lax.broadcasted_iota(jnp.int32, shape, dim)


def _rms(x, g):
    return x * lax.rsqrt(jnp.mean(x * x, axis=-1, keepdims=True) + RMS_EPS) * g


def _rmsnorm_kernel(x_ref, g_ref, o_ref):
    o_ref[...] = _rms(x_ref[...], g_ref[...])


def _rmsnorm_rows(x, g, tm):
    m, d = x.shape
    return pl.pallas_call(
        _rmsnorm_kernel,
        grid=(m // tm,),
        in_specs=[pl.BlockSpec((tm, d), lambda i: (i, 0)), pl.BlockSpec((1, d), lambda i: (0, 0))],
        out_specs=pl.BlockSpec((tm, d), lambda i: (i, 0)),
        out_shape=jax.ShapeDtypeStruct((m, d), F32),
        compiler_params=_cparams(("parallel",)),
        name="rmsnorm",
    )(x, g.reshape(1, d))


def _rk_proj_kernel(xn_ref, prev_ref, mix_ref, wr_ref, wk_ref, wv_ref, w1_ref, w2_ref, w0_ref,
                    a1_ref, a2_ref, a0_ref, g1_ref, g2_ref,
                    r_ref, k_ref, v_ref, wl_ref, a_ref, g_ref):
    xn = xn_ref[...]
    dx = prev_ref[...] - xn
    mix = mix_ref[...]

    def mixed(c):
        return (xn + dx * mix[c:c + 1, :]).astype(BF16)

    def mm(x, w_ref):
        return jnp.dot(x, w_ref[...], preferred_element_type=F32)

    r_ref[...] = mm(mixed(0), wr_ref)
    k_ref[...] = mm(mixed(2), wk_ref)
    v_ref[...] = mm(mixed(3), wv_ref)
    z = w0_ref[...] + mm(jnp.tanh(mm(mixed(1), w1_ref)).astype(BF16), w2_ref)
    wl_ref[...] = -jax.nn.softplus(-z) - 0.5
    a_ref[...] = jax.nn.sigmoid(a0_ref[...] + mm(mm(mixed(4), a1_ref).astype(BF16), a2_ref))
    g_ref[...] = mm(jax.nn.sigmoid(mm(mixed(5), g1_ref)).astype(BF16), g2_ref)


def _pad_to(x, axis, mult):
    pad = (-x.shape[axis]) % mult
    if pad == 0:
        return x
    widths = [(0, 0)] * x.ndim
    widths[axis] = (0, pad)
    return jnp.pad(x, widths)


def _rk_proj(xn, prev, p, tm):
    m, d = xn.shape
    bf = lambda w: w.astype(BF16)
    w1, w2 = bf(_pad_to(p["w1"], 1, LANES)), bf(_pad_to(p["w2"], 0, LANES))
    a1, a2 = bf(_pad_to(p["a1"], 1, LANES)), bf(_pad_to(p["a2"], 0, LANES))
    g1, g2 = bf(_pad_to(p["g1"], 1, LANES)), bf(_pad_to(p["g2"], 0, LANES))
    row = pl.BlockSpec((tm, d), lambda i: (i, 0))
    full = lambda a: pl.BlockSpec(a.shape, lambda i: (0,) * a.ndim)
    args = (xn, prev, p["mix"], bf(p["wr"]), bf(p["wk"]), bf(p["wv"]), w1, w2, p["w0"].reshape(1, d),
            a1, a2, p["a0"].reshape(1, d), g1, g2)
    return pl.pallas_call(
        _rk_proj_kernel,
        grid=(m // tm,),
        in_specs=[row, row] + [full(a) for a in args[2:]],
        out_specs=[row] * 6,
        out_shape=[jax.ShapeDtypeStruct((m, d), F32)] * 6,
        compiler_params=_cparams(("parallel",)),
        name="rk_proj",
    )(*args)


def _rk_scan_kernel(r_ref, k_ref, v_ref, wl_ref, a_ref, g_ref, kk_ref, ka_ref, rk_ref, lg_ref, lb_ref,
                    s0_ref, z_ref, sT_ref, s_scr, *, chunk, t_valid, npair, passes, state_passes):
    c = pl.program_id(2)
    C = chunk
    C2 = 2 * C

    @pl.when(c == 0)
    def _():
        s_scr[...] = s0_ref[0]

    lane = _iota((C, LANES), 1)
    head0 = lane < RK_HEAD
    same_head = ((_iota((LANES, LANES), 0) < RK_HEAD) == (_iota((LANES, LANES), 1) < RK_HEAD))
    bd_ones = same_head.astype(BF16)
    eye = _iota((LANES, LANES), 0) == _iota((LANES, LANES), 1)
    tri_incl = (_iota((C, C), 0) >= _iota((C, C), 1)).astype(BF16)
    ri = _iota((C2, C2), 0)
    ci = _iota((C2, C2), 1)
    same_blk = (ri < C) == (ci < C)
    strict = same_blk & (ri > ci)
    incl = same_blk & (ri >= ci)
    eye2 = jnp.where(ri == ci, 1.0, 0.0)
    valid = _iota((C, LANES), 0) < t_valid
    dp = functools.partial(_dot, passes=passes)
    ds = functools.partial(_dot, passes=state_passes)
    inv_n = 1.0 / RK_HEAD

    def stack(x):
        return jnp.concatenate([jnp.where(head0, x, 0.0), jnp.where(head0, 0.0, x)], axis=0)

    def fold(x):
        return x[:C] + x[C:]

    def each(fn, *lists):
        return [fn(*xs) for xs in zip(*lists)]

    cat0 = lambda *xs: jnp.concatenate(xs, axis=0)
    cat1 = lambda *xs: jnp.concatenate(xs, axis=1)
    lanes = [slice(pr * LANES, (pr + 1) * LANES) for pr in range(npair)]
    r = [r_ref[0, :, ln] for ln in lanes]
    k = [k_ref[0, :, ln] for ln in lanes]
    v = [v_ref[0, :, ln] for ln in lanes]
    a = [a_ref[0, :, ln] for ln in lanes]
    logw = [-jnp.exp(wl_ref[0, :, ln]) for ln in lanes]
    kk = [ki * kk_ref[:, ln] for ki, ln in zip(k, lanes)]
    k2 = [ki * (1.0 + (ai - 1.0) * ka_ref[:, ln]) for ki, ai, ln in zip(k, a, lanes)]
    sums = [_dot_exact_rhs(cat0(kki * kki, ri_ * k2i * rk_ref[:, ln]), bd_ones)
            for kki, ri_, k2i, ln in zip(kk, r, k2, lanes)]
    kk = each(lambda kki, si: kki / jnp.maximum(jnp.sqrt(si[:C]), 1e-12), kk, sums)
    bonus = each(lambda si, vi: si[C:] * vi, sums, v)
    b = each(lambda kki, ai: kki * ai, kk, a)
    if t_valid < C:
        logw = [jnp.where(valid, x, 0.0) for x in logw]
        b = [jnp.where(valid, x, 0.0) for x in b]
        k2 = [jnp.where(valid, x, 0.0) for x in k2]

    cum = [_dot_exact_lhs(tri_incl, x) for x in logw]
    gam = [jnp.exp(x) for x in cum]
    inv_gam = [jnp.exp(-x) for x in cum]
    gam_prev = each(lambda cu, lw: jnp.exp(cu - lw), cum, logw)
    g_last = [x[C - 1:C, :] for x in gam]

    bt = each(lambda x, ig: x * ig, b, inv_gam)
    kt = each(lambda x, ig: x * ig, k2, inv_gam)
    ats = each(lambda kki, gp: stack(-kki * gp), kk, gam_prev)
    bts, kts, vs = [stack(x) for x in bt], [stack(x) for x in kt], [stack(x) for x in v]
    rts = each(lambda ri_, gi: stack(ri_ * gi), r, gam)
    if C2 % LANES == 0:
        gram = each(lambda at_, rt_, bt_, kt_: dp(cat0(at_, rt_), cat0(bt_, kt_), 1, 1), ats, rts, bts, kts)
        g_ab, g_ak = [x[:C2, :C2] for x in gram], [x[:C2, C2:] for x in gram]
        g_rb, g_rk = [x[C2:, :C2] for x in gram], [x[C2:, C2:] for x in gram]
    else:
        g_ab, g_ak = each(lambda x, y_: dp(x, y_, 1, 1), ats, bts), each(lambda x, y_: dp(x, y_, 1, 1), ats, kts)
        g_rb, g_rk = each(lambda x, y_: dp(x, y_, 1, 1), rts, bts), each(lambda x, y_: dp(x, y_, 1, 1), rts, kts)
    a_ak = [jnp.where(strict, x, 0.0) for x in g_ak]
    m_rb = [jnp.where(incl, x, 0.0) for x in g_rb]
    m_rk = [jnp.where(incl, x, 0.0) for x in g_rk]

    x = [jnp.where(strict, g, 0.0) for g in g_ab]
    t_inv = [eye2 + xi for xi in x]
    for _ in range(int(math.log2(C)) - 1):
        x = [dp(xi, xi) for xi in x]
        t_inv = each(lambda ti, xi: ti + dp(ti, xi), t_inv, x)

    ws = each(dp, a_ak, vs)
    tp = each(lambda ti, at_, wi: dp(ti, cat1(at_, wi)), t_inv, ats, ws)
    mq = each(dp, m_rb, tp)
    mv = each(dp, m_rk, vs)
    q1 = each(lambda rt_, m_: fold(rt_ + m_[:, :LANES]), rts, mq)
    q2 = each(lambda m_, mv_: fold(m_[:, LANES:] + mv_), mq, mv)
    p1 = [fold(t[:, :LANES]) for t in tp]
    p2 = [fold(t[:, LANES:]) for t in tp]

    s = [s_scr[pr] for pr in range(npair)]
    y = each(lambda q1_, s_, q2_: ds(q1_, s_, 1, 1) + q2_, q1, s, q2)
    bp = each(lambda x_, gl: x_ * gl, bt, g_last)
    kp = each(lambda x_, gl: x_ * gl, kt, g_last)
    gm = each(lambda gl, p1_, bp_: jnp.where(eye, gl, 0.0) + jnp.where(same_head, dp(p1_, bp_, 0, 0), 0.0),
              g_last, p1, bp)
    hm = each(lambda p2_, v_, bp_, kp_: jnp.where(same_head, dp(cat0(p2_, v_), cat0(bp_, kp_), 0, 0), 0.0),
              p2, v, bp, kp)
    s_new = each(lambda s_, gm_, hm_: ds(s_, gm_) + hm_, s, gm, hm)
    for pr in range(npair):
        s_scr[pr] = s_new[pr]

    mu = [_dot_exact_rhs(yi, bd_ones) * inv_n for yi in y]
    d = each(lambda yi, mi: yi - mi, y, mu)
    var = [_dot_exact_rhs(di * di, bd_ones) * inv_n for di in d]
    for pr, ln in enumerate(lanes):
        yn = d[pr] * lax.rsqrt(var[pr] + GN_EPS) * lg_ref[:, ln] + lb_ref[:, ln]
        z_ref[0, :, ln] = (yn + bonus[pr]) * g_ref[0, :, ln]

    @pl.when(c == pl.num_programs(2) - 1)
    def _():
        sT_ref[0] = s_scr[...]


def _pair_state(s):
    bsz, h, n, _ = s.shape
    s = s.reshape(bsz, h // 2, 2, n, n)
    z = jnp.zeros_like(s[:, :, 0])
    top = jnp.concatenate([s[:, :, 0], z], axis=-1)
    bot = jnp.concatenate([z, s[:, :, 1]], axis=-1)
    return jnp.concatenate([top, bot], axis=-2)


def _unpair_state(sp):
    n = sp.shape[-1] // 2
    s = jnp.stack([sp[:, :, :n, :n], sp[:, :, n:, n:]], axis=2)
    return s.reshape(sp.shape[0], sp.shape[1] * 2, n, n)


def _rk_scan(r, k, v, wl, a, g, p, s0, chunk, t_valid, npair=8, passes=1, state_passes=3):
    bsz, t, d = r.shape
    width = npair * LANES
    ngroup = d // width
    nchunk = t // chunk
    tok = pl.BlockSpec((1, chunk, width), lambda b, h, c: (b, c, h))
    par = pl.BlockSpec((1, width), lambda b, h, c: (0, h))
    st = pl.BlockSpec((1, npair, LANES, LANES), lambda b, h, c: (b, h, 0, 0))
    vec = lambda x: x.reshape(1, d)
    z, s_t = pl.pallas_call(
        functools.partial(_rk_scan_kernel, chunk=chunk, t_valid=t_valid, npair=npair, passes=passes,
                          state_passes=state_passes),
        grid=(bsz, ngroup, nchunk),
        in_specs=[tok] * 6 + [par] * 5 + [st],
        out_specs=[tok, st],
        out_shape=[jax.ShapeDtypeStruct((bsz, t, d), F32),
                   jax.ShapeDtypeStruct((bsz, d // LANES, LANES, LANES), F32)],
        scratch_shapes=[pltpu.VMEM((npair, LANES, LANES), F32)],
        compiler_params=_cparams(("parallel", "parallel", "arbitrary")),
        name="rk_scan",
    )(r, k, v, wl, a, g, vec(p["k_k"]), vec(p["k_a"]), vec(p["r_k"]), vec(p["lnx_g"]), vec(p["lnx_b"]),
      _pair_state(s0))
    return z, _unpair_state(s_t)


def _mm_res_kernel(x_ref, w_ref, res_ref, o_ref):
    o_ref[...] = res_ref[...] + jnp.dot(x_ref[...].astype(BF16), w_ref[...], preferred_element_type=F32)


def _mm_res(x, w, res, tm):
    m, kd = x.shape
    n = w.shape[1]
    return pl.pallas_call(
        _mm_res_kernel,
        grid=(m // tm,),
        in_specs=[pl.BlockSpec((tm, kd), lambda i: (i, 0)), pl.BlockSpec((kd, n), lambda i: (0, 0)),
                  pl.BlockSpec((tm, n), lambda i: (i, 0))],
        out_specs=pl.BlockSpec((tm, n), lambda i: (i, 0)),
        out_shape=jax.ShapeDtypeStruct((m, n), F32),
        compiler_params=_cparams(("parallel",)),
        name="mm_res",
    )(x, w.astype(BF16), res)


def _ffn_kernel(x_ref, g_ref, wg_ref, wu_ref, wd_ref, o_ref):
    x = x_ref[...]
    xn = _rms(x, g_ref[...]).astype(BF16)
    gt = jnp.dot(xn, wg_ref[...], preferred_element_type=F32)
    up = jnp.dot(xn, wu_ref[...], preferred_element_type=F32)
    h = (jax.nn.silu(gt) * up).astype(BF16)
    o_ref[...] = x + jnp.dot(h, wd_ref[...], preferred_element_type=F32)


def _ffn(x, g, w_gu, w_down, tm):
    m, d = x.shape
    ff = w_down.shape[0]
    wg, wu = w_gu[:, :ff].astype(BF16), w_gu[:, ff:].astype(BF16)
    full = lambda a: pl.BlockSpec(a.shape, lambda i: (0,) * a.ndim)
    row = pl.BlockSpec((tm, d), lambda i: (i, 0))
    wd = w_down.astype(BF16)
    return pl.pallas_call(
        _ffn_kernel,
        grid=(m // tm,),
        in_specs=[row, pl.BlockSpec((1, d), lambda i: (0, 0)), full(wg), full(wu), full(wd)],
        out_specs=row,
        out_shape=jax.ShapeDtypeStruct((m, d), F32),
        compiler_params=_cparams(("parallel",)),
        name="ffn",
    )(x, g.reshape(1, d), wg, wu, wd)


def _moe_kernel(x_ref, g_ref, rt_ref, wg_ref, wu_ref, wd_ref, o_ref, xn_scr, gate_scr):
    e = pl.program_id(1)

    @pl.when(e == 0)
    def _():
        x = x_ref[...]
        xn = _rms(x, g_ref[...])
        xn_scr[...] = xn.astype(BF16)
        logits = _dot(xn, rt_ref[...], passes=6)
        lane = _iota(logits.shape, 1)
        logits = jnp.where(lane < N_EXPERTS, logits, NEG_INF)
        v1 = jnp.max(logits, axis=-1, keepdims=True)
        i1 = jnp.min(jnp.where(logits == v1, lane, LANES), axis=-1, keepdims=True)
        rest = jnp.where(lane == i1, NEG_INF, logits)
        v2 = jnp.max(rest, axis=-1, keepdims=True)
        i2 = jnp.min(jnp.where(rest == v2, lane, LANES), axis=-1, keepdims=True)
        e2 = jnp.exp(v2 - v1)
        w1 = 1.0 / (1.0 + e2)
        w2 = e2 / (1.0 + e2)
        gate_scr[...] = jnp.where(lane == i1, w1, 0.0) + jnp.where(lane == i2, w2, 0.0)
        o_ref[...] = x

    xn = xn_scr[...]
    gt = jnp.dot(xn, wg_ref[0], preferred_element_type=F32)
    up = jnp.dot(xn, wu_ref[0], preferred_element_type=F32)
    h = (jax.nn.silu(gt) * up).astype(BF16)
    y = jnp.dot(h, wd_ref[0], preferred_element_type=F32)
    gates = gate_scr[...]
    gate_e = jnp.sum(jnp.where(_iota(gates.shape, 1) == e, gates, 0.0), axis=-1, keepdims=True)
    o_ref[...] += gate_e * y


def _moe(x, g, router, w_gu, w_down, tm):
    m, d = x.shape
    ne, ffe = w_down.shape[0], w_down.shape[1]
    wg, wu = w_gu[:, :, :ffe].astype(BF16), w_gu[:, :, ffe:].astype(BF16)
    wd = w_down.astype(BF16)
    rt = _pad_to(router, 1, LANES)
    row = pl.BlockSpec((tm, d), lambda i, e: (i, 0))
    return pl.pallas_call(
        _moe_kernel,
        grid=(m // tm, ne),
        in_specs=[row, pl.BlockSpec((1, d), lambda i, e: (0, 0)), pl.BlockSpec(rt.shape, lambda i, e: (0, 0)),
                  pl.BlockSpec((1, d, ffe), lambda i, e: (e, 0, 0)),
                  pl.BlockSpec((1, d, ffe), lambda i, e: (e, 0, 0)),
                  pl.BlockSpec((1, ffe, d), lambda i, e: (e, 0, 0))],
        out_specs=row,
        out_shape=jax.ShapeDtypeStruct((m, d), F32),
        scratch_shapes=[pltpu.VMEM((tm, d), BF16), pltpu.VMEM((tm, LANES), F32)],
        compiler_params=_cparams(("parallel", "arbitrary")),
        name="moe",
    )(x, g.reshape(1, d), rt, wg, wu, wd)


def _rope_tables(pos):
    inv = ROPE_THETA ** (-jnp.arange(0, ROT_DIM, 2, dtype=F32) / ROT_DIM)
    ang = pos.astype(F32)[:, None] * inv[None, :]
    cos, sin = jnp.cos(ang), jnp.sin(ang)
    ones = jnp.ones((pos.shape[0], AT_HEAD_DIM - ROT_DIM), F32)
    cos_g = jnp.concatenate([cos, cos, ones], axis=1)
    sin_g = jnp.concatenate([-sin, sin, 0.0 * ones], axis=1)
    return jnp.tile(cos_g, (1, 2)), jnp.tile(sin_g, (1, 2))


def _norm_rope(x, gain, cos, sin):
    lane = _iota(x.shape, 1)
    first = lane < AT_HEAD_DIM
    sq = x * x
    s0 = jnp.sum(jnp.where(first, sq, 0.0), axis=-1, keepdims=True)
    s1 = jnp.sum(jnp.where(first, 0.0, sq), axis=-1, keepdims=True)
    ms = jnp.where(first, s0, s1) * (1.0 / AT_HEAD_DIM)
    y = x * lax.rsqrt(ms + RMS_EPS) * gain
    half = ROT_DIM // 2
    in_lo = (lane % AT_HEAD_DIM) < half
    partner = jnp.where(in_lo, pltpu.roll(y, LANES - half, axis=1), pltpu.roll(y, half, axis=1))
    return y * cos + partner * sin


def _qkv_kernel(x_ref, g_ref, w_ref, qg_ref, kg_ref, cos_ref, sin_ref, *outs, transpose_k):
    j = pl.program_id(1)
    xn_scr = outs[-1]
    nh = x_ref.shape[1] // LANES

    @pl.when(j == 0)
    def _():
        xn_scr[...] = _rms(x_ref[...], g_ref[...]).astype(BF16)

    y = jnp.dot(xn_scr[...], w_ref[...], preferred_element_type=F32)
    heads = [slice(h * LANES, (h + 1) * LANES) for h in range(nh)]

    @pl.when(j == 0)
    def _():
        for hs in heads:
            outs[0][:, hs] = _norm_rope(y[:, hs], qg_ref[...], cos_ref[...], sin_ref[...])

    @pl.when(j == 1)
    def _():
        for hs in heads:
            kh = _norm_rope(y[:, hs], kg_ref[...], cos_ref[...], sin_ref[...])
            if transpose_k:
                kt = kh.T
                outs[1][0, hs, :] = kt
                outs[2][0, 0, hs, :] = kt.astype(BF16)
            else:
                outs[1][:, hs] = kh

    @pl.when(j == 2)
    def _():
        if transpose_k:
            outs[3][...] = y
            outs[4][...] = y.astype(BF16)
        else:
            outs[2][...] = y


def _qkv_proj(x, g, w_in, qn_g, kn_g, cos_t, sin_t, tm, seq, transpose_k):
    m, d = x.shape
    nt = cos_t.shape[0] // tm
    row = pl.BlockSpec((tm, d), lambda i, j: (i, 0))
    vec = pl.BlockSpec((1, LANES), lambda i, j: (0, 0))
    tab = pl.BlockSpec((tm, LANES), lambda i, j: (i % nt, 0))
    f = jax.ShapeDtypeStruct
    if transpose_k:
        bsz, ns = m // seq, seq // tm
        out_shape = [f((m, d), F32), f((bsz, d, seq), F32), f((bsz, ns, d, tm), BF16), f((m, d), F32),
                     f((m, d), BF16)]
        out_specs = [row, pl.BlockSpec((1, d, tm), lambda i, j: (i // ns, 0, i % ns)),
                     pl.BlockSpec((1, 1, d, tm), lambda i, j: (i // ns, i % ns, 0, 0)), row, row]
    else:
        out_shape = [f((m, d), F32)] * 3
        out_specs = [row] * 3
    return pl.pallas_call(
        functools.partial(_qkv_kernel, transpose_k=transpose_k),
        grid=(m // tm, 3),
        in_specs=[row, pl.BlockSpec((1, d), lambda i, j: (0, 0)), pl.BlockSpec((d, d), lambda i, j: (0, j)),
                  vec, vec, tab, tab],
        out_specs=out_specs,
        out_shape=out_shape,
        scratch_shapes=[pltpu.VMEM((tm, d), BF16)],
        compiler_params=_cparams(("parallel", "arbitrary")),
        name="qkv_proj",
    )(x, g.reshape(1, d), w_in.astype(BF16), jnp.tile(qn_g, 2).reshape(1, LANES),
      jnp.tile(kn_g, 2).reshape(1, LANES), cos_t, sin_t)


def _diff_finish(o1, o2, lam, sub_g, lam_init):
    o = o1 - lam * o2
    o = o * lax.rsqrt(jnp.mean(o * o, axis=-1, keepdims=True) + RMS_EPS) * sub_g
    return o * (1.0 - lam_init)


def _branch_rows(q):
    first = _iota(q.shape, 1) < AT_HEAD_DIM
    return jnp.concatenate([jnp.where(first, q, 0.0), jnp.where(first, 0.0, q)], axis=0)


def _attn_prompt_kernel(lam_ref, q_ref, kt_ref, v_ref, sg_ref, o_ref, m_ref, l_ref, acc_ref,
                        *, tq, tk, lam_init):
    qi = pl.program_id(2)
    qs = _branch_rows(q_ref[0] * (ATTN_SCALE * LOG2E)).astype(BF16)
    m_ref[...] = jnp.full(m_ref.shape, NEG_INF, F32)
    l_ref[...] = jnp.zeros(l_ref.shape, F32)
    acc_ref[...] = jnp.zeros(acc_ref.shape, F32)

    def tile(j, masked):
        s = _dg(qs, kt_ref[0, j], 1, 0)
        if masked:
            qpos = qi * tq + (_iota(s.shape, 0) % tq)
            kpos = j * tk + _iota(s.shape, 1)
            s = jnp.where(kpos <= qpos, s, NEG_INF)
        m_prev = m_ref[...]
        m_new = jnp.maximum(m_prev, jnp.max(s, axis=-1, keepdims=True))
        alpha = jnp.exp2(m_prev - m_new)
        p = jnp.exp2(s - jnp.concatenate([m_new] * (tk // LANES), axis=1))
        l_ref[...] = alpha * l_ref[...] + jnp.sum(p, axis=-1, keepdims=True)
        acc_ref[...] = alpha * acc_ref[...] + jnp.dot(p.astype(BF16), v_ref[0, j],
                                                      preferred_element_type=F32)
        m_ref[...] = m_new

    j_last = (qi * tq + (tq - 1)) // tk

    def body(j, carry):
        tile(j, False)
        return carry

    lax.fori_loop(0, j_last, body, 0)
    tile(j_last, True)
    o = acc_ref[...] / l_ref[...]
    o_ref[0] = _diff_finish(o[:tq], o[tq:], lam_ref[0], sg_ref[...], lam_init)


def _attn_prompt(q, kt_bf, v_bf, lam, sub_g, lam_init, bsz, seq, tq):
    d = q.shape[1]
    nh = d // LANES
    nk, tk = kt_bf.shape[1], kt_bf.shape[3]
    out = pl.pallas_call(
        functools.partial(_attn_prompt_kernel, tq=tq, tk=tk, lam_init=lam_init),
        grid=(bsz, nh, seq // tq),
        in_specs=[pl.BlockSpec(memory_space=pltpu.SMEM),
                  pl.BlockSpec((1, tq, LANES), lambda b, h, i: (b, i, h)),
                  pl.BlockSpec((1, nk, LANES, tk), lambda b, h, i: (b, 0, h, 0)),
                  pl.BlockSpec((1, nk, tk, LANES), lambda b, h, i: (b, 0, 0, h)),
                  pl.BlockSpec((1, LANES), lambda b, h, i: (0, 0))],
        out_specs=pl.BlockSpec((1, tq, LANES), lambda b, h, i: (b, i, h)),
        out_shape=jax.ShapeDtypeStruct((bsz, seq, d), F32),
        scratch_shapes=[pltpu.VMEM((2 * tq, LANES), F32), pltpu.VMEM((2 * tq, LANES), F32),
                        pltpu.VMEM((2 * tq, LANES), F32)],
        compiler_params=_cparams(("parallel", "parallel", "arbitrary")),
        name="attn_prompt",
    )(lam.reshape(1), q.reshape(bsz, seq, d), kt_bf, v_bf.reshape(bsz, nk, tk, d), sub_g.reshape(1, LANES))
    return out.reshape(bsz * seq, d)


def _attn_decode_kernel(pt_ref, lam_ref, q_ref, kn_ref, vn_ref, sg_ref, *rest, nq, nh, pp, lam_init):
    k_refs, v_refs = rest[:pp], rest[pp:2 * pp]
    o_ref, m_ref, l_ref, acc_ref = rest[2 * pp:]
    step = pl.program_id(1)
    rows = q_ref.shape[1] // nh
    rpb = rows // 2
    page = k_refs[0].shape[3]

    @pl.when(step == 0)
    def _():
        m_ref[...] = jnp.full(m_ref.shape, NEG_INF, F32)
        l_ref[...] = jnp.zeros(l_ref.shape, F32)
        acc_ref[...] = jnp.zeros(acc_ref.shape, F32)

    q = q_ref[0] * ATTN_SCALE
    keep = ((_iota(q.shape, 0) % rows) < rpb) == (_iota(q.shape, 1) < AT_HEAD_DIM)
    qs = jnp.where(keep, q, 0.0)
    hrows = [slice(h * rows, (h + 1) * rows) for h in range(nh)]
    bf = lambda x: x.astype(BF16)

    def update(s, pv_fn):
        m_prev = m_ref[...]
        m_new = jnp.maximum(m_prev, jnp.max(s, axis=-1, keepdims=True))
        alpha = jnp.exp(m_prev - m_new)
        p = jnp.exp(s - m_new)
        l_ref[...] = alpha * l_ref[...] + jnp.sum(p, axis=-1, keepdims=True)
        acc_ref[...] = alpha * acc_ref[...] + pv_fn(p)
        m_ref[...] = m_new

    s = jnp.concatenate(
        [jnp.concatenate([_dg(bf(qs[hr]), bf(k_refs[i][0, 0, h * LANES:(h + 1) * LANES, :]), 1, 0)
                          for h, hr in enumerate(hrows)], axis=0) for i in range(pp)], axis=1)

    def pv_pages(p):
        outs = []
        for h, hr in enumerate(hrows):
            o = None
            for i in range(pp):
                vh = bf(v_refs[i][0, 0, pl.ds(h, page, stride=nh), :])
                t = jnp.dot(bf(p[hr, i * page:(i + 1) * page]), vh, preferred_element_type=F32)
                o = t if o is None else o + t
            outs.append(o)
        return jnp.concatenate(outs, axis=0)

    update(s, pv_pages)

    @pl.when(step == pl.num_programs(1) - 1)
    def _():
        nk = kn_ref.shape[2]
        sn = jnp.concatenate([_dg(bf(qs[hr]), bf(kn_ref[0, h]), 1, 1) for h, hr in enumerate(hrows)],
                             axis=0)
        col = _iota(sn.shape, 1)
        causal = (col <= (_iota(sn.shape, 0) % rpb)) & (col < nq)
        update(jnp.where(causal, sn, NEG_INF),
               lambda p: jnp.concatenate([jnp.dot(bf(p[hr]), bf(vn_ref[0, h]), preferred_element_type=F32)
                                          for h, hr in enumerate(hrows)], axis=0))
        o = acc_ref[...] / l_ref[...]
        total = o.shape[0]
        o_ref[0] = _diff_finish(o, pltpu.roll(o, total - rpb, axis=0), lam_ref[0], sg_ref[...], lam_init)


def _attn_decode(q, k, v, cache_k, cache_v, layer, page_table, lam, sub_g, lam_init, bsz, nq, pp):
    d = q.shape[1]
    nh = d // LANES
    n_pages = page_table.shape[1]
    n_layers, n_pool, page = cache_k.shape[:3]
    kt = jnp.transpose(cache_k, (0, 1, 3, 4, 5, 2)).reshape(n_layers, n_pool, d, page)
    vv = cache_v.reshape(n_layers, n_pool, page * nh, LANES)
    rpb = -(-nq // 4) * 4
    heads_first = lambda x: jnp.transpose(x.reshape(bsz, nq, nh, LANES), (0, 2, 1, 3))
    qh = _pad_to(heads_first(q), 2, rpb)
    qh = jnp.concatenate([qh, qh], axis=2).reshape(bsz, nh * 2 * rpb, LANES)
    kh = _pad_to(heads_first(k), 2, 16)
    vh = _pad_to(heads_first(v), 2, 16)
    rows = nh * 2 * rpb
    qspec = pl.BlockSpec((1, rows, LANES), lambda b, s, pt: (b, 0, 0))
    nspec = pl.BlockSpec((1, nh, kh.shape[2], LANES), lambda b, s, pt: (b, 0, 0, 0))

    def page_spec(i, nrows):
        return pl.BlockSpec((1, 1, nrows, LANES), lambda b, s, pt: (layer, pt[b, s * pp + i], 0, 0))

    grid_spec = pltpu.PrefetchScalarGridSpec(
        num_scalar_prefetch=1,
        grid=(bsz, n_pages // pp),
        in_specs=([pl.BlockSpec(memory_space=pltpu.SMEM), qspec, nspec, nspec,
                   pl.BlockSpec((1, LANES), lambda b, s, pt: (0, 0))]
                  + [page_spec(i, d) for i in range(pp)] + [page_spec(i, page * nh) for i in range(pp)]),
        out_specs=qspec,
        scratch_shapes=[pltpu.VMEM((rows, 1), F32), pltpu.VMEM((rows, 1), F32),
                        pltpu.VMEM((rows, LANES), F32)],
    )
    out = pl.pallas_call(
        functools.partial(_attn_decode_kernel, nq=nq, nh=nh, pp=pp, lam_init=lam_init),
        grid_spec=grid_spec,
        out_shape=jax.ShapeDtypeStruct((bsz, rows, LANES), F32),
        compiler_params=_cparams(("parallel", "arbitrary")),
        name="attn_decode",
    )(page_table, lam.reshape(1), qh, kh, vh, sub_g.reshape(1, LANES), *([kt] * pp), *([vv] * pp))
    out = out.reshape(bsz, nh, 2, rpb, LANES)[:, :, 0, :nq]
    return jnp.transpose(out, (0, 2, 1, 3)).reshape(bsz * nq, d)


def _row_tile(m, want):
    t = min(m, want)
    while m % t:
        t //= 2
    return t


def _rwkv_layer(x, shift_prev, s0, norm_g, p):
    bsz, t, d = x.shape
    m = bsz * t
    xn = _rmsnorm_rows(x.reshape(m, d), norm_g, _row_tile(m, 512)).reshape(bsz, t, d)
    prev = jnp.concatenate([shift_prev[:, None, :], xn[:, :-1]], axis=1)
    r, k, v, wl, a, g = _rk_proj(xn.reshape(m, d), prev.reshape(m, d), p, _row_tile(m, 256))
    chunk = 64 if t % 64 == 0 else 8
    t_pad = -(-t // chunk) * chunk
    seqs = [u.reshape(bsz, t, d) for u in (r, k, v, wl, a, g)]
    if t_pad != t:
        seqs = [_pad_to(u, 1, chunk) for u in seqs]
    z, s_t = _rk_scan(*seqs, p, s0, chunk, t if t_pad != t else chunk)
    z = z[:, :t].reshape(m, d)
    out = _mm_res(z, p["wo"], x.reshape(m, d), _row_tile(m, 512))
    return out.reshape(bsz, t, d), xn[:, -1], s_t


def _attn_layer(x, pos, norm_g, w_in, qn_g, kn_g, lam, lam_init, sub_g, wo, decode=None):
    bsz, t, d = x.shape
    m = bsz * t
    nh = d // LANES
    x2 = x.reshape(m, d)
    cos_t, sin_t = _rope_tables(pos)
    if decode is None:
        tm = _row_tile(t, 512)
        q, k_t, kt_bf, v, v_bf = _qkv_proj(x2, norm_g, w_in, qn_g, kn_g, cos_t, sin_t, tm, t, True)
        o = _attn_prompt(q, kt_bf, v_bf, lam, sub_g, lam_init, bsz, t, _row_tile(t, 256))
        k_out = jnp.transpose(k_t.reshape(bsz, nh, 2, AT_HEAD_DIM, t), (0, 4, 1, 2, 3))
    else:
        cache_k, cache_v, layer, page_table = decode
        cos_t, sin_t = jnp.tile(cos_t, (bsz, 1)), jnp.tile(sin_t, (bsz, 1))
        q, k, v = _qkv_proj(x2, norm_g, w_in, qn_g, kn_g, cos_t, sin_t, m, t, False)
        pp = _row_tile(page_table.shape[1], 8)
        o = _attn_decode(q, k, v, cache_k, cache_v, layer, page_table, lam, sub_g, lam_init, bsz, t, pp)
        k_out = k.reshape(bsz, t, nh, 2, AT_HEAD_DIM)
    y = _mm_res(o, wo, x2, _row_tile(m, 512)).reshape(bsz, t, d)
    return y, k_out, v.reshape(bsz, t, nh, 2 * AT_HEAD_DIM)


def kernel(x_prompt, x_sample, state_wkv, state_shift, cache_k, cache_v, page_table,
           rk_norm_g, rk_mix, rk_wr, rk_wk, rk_wv, rk_w0, rk_w1, rk_w2, rk_a0, rk_a1, rk_a2,
           rk_g1, rk_g2, rk_k_k, rk_k_a, rk_r_k, rk_lnx_g, rk_lnx_b, rk_wo,
           ffn_norm_g, ffn_w_gu, ffn_w_down,
           at_norm_g, at_w_in, at_q_norm_g, at_k_norm_g, at_lq1, at_lk1, at_lq2, at_lk2,
           at_subln_g, at_wo,
           moe_norm_g, moe_router, moe_w_gu, moe_w_down):
    yp, ys = x_prompt, x_sample
    bp, tp, d = x_prompt.shape
    bs, ts, _ = x_sample.shape
    depth = rk_norm_g.shape[0] + at_norm_g.shape[0]
    past = page_table.shape[1] * cache_k.shape[2]
    pos_p = jnp.arange(tp)
    pos_s = past + jnp.arange(ts)
    nheads = d // RK_HEAD
    p_wkv, p_shift, p_k, p_v = [], [], [], []
    s_wkv, s_shift, s_k, s_v = [], [], [], []
    for i in range(depth):
        j = i // 2
        if i % 2 == 0:
            p = dict(mix=rk_mix[j], wr=rk_wr[j], wk=rk_wk[j], wv=rk_wv[j], w0=rk_w0[j], w1=rk_w1[j],
                     w2=rk_w2[j], a0=rk_a0[j], a1=rk_a1[j], a2=rk_a2[j], g1=rk_g1[j], g2=rk_g2[j],
                     k_k=rk_k_k[j], k_a=rk_k_a[j], r_k=rk_r_k[j].reshape(-1), lnx_g=rk_lnx_g[j],
                     lnx_b=rk_lnx_b[j], wo=rk_wo[j])
            yp, sh, st = _rwkv_layer(yp, jnp.zeros((bp, d), F32),
                                     jnp.zeros((bp, nheads, RK_HEAD, RK_HEAD), F32), rk_norm_g[j], p)
            p_shift.append(sh)
            p_wkv.append(st)
            ys, sh, st = _rwkv_layer(ys, state_shift[j], state_wkv[j], rk_norm_g[j], p)
            s_shift.append(sh)
            s_wkv.append(st)
            yp = _ffn(yp.reshape(bp * tp, d), ffn_norm_g[j], ffn_w_gu[j], ffn_w_down[j],
                      _row_tile(bp * tp, 512)).reshape(bp, tp, d)
            ys = _ffn(ys.reshape(bs * ts, d), ffn_norm_g[j], ffn_w_gu[j], ffn_w_down[j],
                      _row_tile(bs * ts, 512)).reshape(bs, ts, d)
        else:
            lam_init = 0.8 - 0.6 * math.exp(-0.3 * i)
            lam = (jnp.exp(jnp.sum(at_lq1[j] * at_lk1[j])) - jnp.exp(jnp.sum(at_lq2[j] * at_lk2[j]))
                   + lam_init)
            common = (at_norm_g[j], at_w_in[j], at_q_norm_g[j], at_k_norm_g[j], lam, lam_init,
                      at_subln_g[j], at_wo[j])
            yp, k_new, v_new = _attn_layer(yp, pos_p, *common)
            p_k.append(k_new)
            p_v.append(v_new)
            ys, k_new, v_new = _attn_layer(ys, pos_s, *common, decode=(cache_k, cache_v, j, page_table))
            s_k.append(k_new)
            s_v.append(v_new)
            yp = _moe(yp.reshape(bp * tp, d), moe_norm_g[j], moe_router[j], moe_w_gu[j], moe_w_down[j],
                      _row_tile(bp * tp, 1024)).reshape(bp, tp, d)
            ys = _moe(ys.reshape(bs * ts, d), moe_norm_g[j], moe_router[j], moe_w_gu[j], moe_w_down[j],
                      _row_tile(bs * ts, 1024)).reshape(bs, ts, d)
    return (yp, ys, jnp.stack(p_wkv), jnp.stack(p_shift), jnp.stack(p_k), jnp.stack(p_v),
            jnp.stack(s_wkv), jnp.stack(s_shift), jnp.stack(s_k), jnp.stack(s_v))
```

```python
import functools
import math

import jax
import jax.numpy as jnp
from jax import lax
from jax.experimental import pallas as pl
from jax.experimental.pallas import tpu as pltpu

F32 = jnp.float32
BF16 = jnp.bfloat16

RK_HEAD = 64
GN_EPS = 64e-5
AT_HEAD_DIM = 64
ROT_DIM = AT_HEAD_DIM // 4
ROPE_THETA = 500000.0
ATTN_SCALE = AT_HEAD_DIM ** -0.5
LOG2E = 1.4426950408889634
NEG_INF = -1e30
N_EXPERTS = 8
RMS_EPS = 1e-6
LANES = 128
VMEM_LIMIT = 56 * 1024 * 1024


def _cparams(sem):
    return pltpu.CompilerParams(dimension_semantics=sem, vmem_limit_bytes=VMEM_LIMIT)


def _dg(a, b, ca, cb):
    return lax.dot_general(a, b, (((ca,), (cb,)), ((), ())), preferred_element_type=F32)


def _split(x, n):
    parts = []
    for i in range(n):
        p = x.astype(BF16)
        parts.append(p)
        if i + 1 < n:
            x = x - p.astype(F32)
    return parts


def _dot(a, b, ca=1, cb=0, passes=3):
    if passes == 1:
        return _dg(a.astype(BF16), b.astype(BF16), ca, cb)
    if passes == 3:
        ah, al = _split(a, 2)
        bh, bl = _split(b, 2)
        return _dg(ah, bh, ca, cb) + (_dg(ah, bl, ca, cb) + _dg(al, bh, ca, cb))
    ah, am, al = _split(a, 3)
    bh, bm, bl = _split(b, 3)
    return (_dg(ah, bh, ca, cb) + (_dg(ah, bm, ca, cb) + _dg(am, bh, ca, cb))
            + (_dg(am, bm, ca, cb) + _dg(ah, bl, ca, cb) + _dg(al, bh, ca, cb)))


def _dot_exact_rhs(a, b_bf16, ca=1, cb=0):
    ah, al = _split(a, 2)
    return _dg(ah, b_bf16, ca, cb) + _dg(al, b_bf16, ca, cb)


def _dot_exact_lhs(a_bf16, b, ca=1, cb=0):
    bh, bl = _split(b, 2)
    return _dg(a_bf16, bh, ca, cb) + _dg(a_bf16, bl, ca, cb)


def _iota(shape, dim):
    return lax.broadcasted_iota(jnp.int32, shape, dim)


def _rms(x, g):
    return x * lax.rsqrt(jnp.mean(x * x, axis=-1, keepdims=True) + RMS_EPS) * g


def _rmsnorm_kernel(x_ref, g_ref, o_ref):
    o_ref[...] = _rms(x_ref[...], g_ref[...])


def _rmsnorm_rows(x, g, tm):
    m, d = x.shape
    return pl.pallas_call(
        _rmsnorm_kernel,
        grid=(m // tm,),
        in_specs=[pl.BlockSpec((tm, d), lambda i: (i, 0)), pl.BlockSpec((1, d), lambda i: (0, 0))],
        out_specs=pl.BlockSpec((tm, d), lambda i: (i, 0)),
        out_shape=jax.ShapeDtypeStruct((m, d), F32),
        compiler_params=_cparams(("parallel",)),
        name="rmsnorm",
    )(x, g.reshape(1, d))


def _rk_proj_kernel(xn_ref, prev_ref, mix_ref, wr_ref, wk_ref, wv_ref, w1_ref, w2_ref, w0_ref,
                    a1_ref, a2_ref, a0_ref, g1_ref, g2_ref,
                    r_ref, k_ref, v_ref, wl_ref, a_ref, g_ref):
    xn = xn_ref[...]
    dx = prev_ref[...] - xn
    mix = mix_ref[...]

    def mixed(c):
        return (xn + dx * mix[c:c + 1, :]).astype(BF16)

    def mm(x, w_ref):
        return jnp.dot(x, w_ref[...], preferred_element_type=F32)

    r_ref[...] = mm(mixed(0), wr_ref)
    k_ref[...] = mm(mixed(2), wk_ref)
    v_ref[...] = mm(mixed(3), wv_ref)
    z = w0_ref[...] + mm(jnp.tanh(mm(mixed(1), w1_ref)).astype(BF16), w2_ref)
    wl_ref[...] = -jax.nn.softplus(-z) - 0.5
    a_ref[...] = jax.nn.sigmoid(a0_ref[...] + mm(mm(mixed(4), a1_ref).astype(BF16), a2_ref))
    g_ref[...] = mm(jax.nn.sigmoid(mm(mixed(5), g1_ref)).astype(BF16), g2_ref)


def _pad_to(x, axis, mult):
    pad = (-x.shape[axis]) % mult
    if pad == 0:
        return x
    widths = [(0, 0)] * x.ndim
    widths[axis] = (0, pad)
    return jnp.pad(x, widths)


def _rk_proj(xn, prev, p, tm):
    m, d = xn.shape
    bf = lambda w: w.astype(BF16)
    w1, w2 = bf(_pad_to(p["w1"], 1, LANES)), bf(_pad_to(p["w2"], 0, LANES))
    a1, a2 = bf(_pad_to(p["a1"], 1, LANES)), bf(_pad_to(p["a2"], 0, LANES))
    g1, g2 = bf(_pad_to(p["g1"], 1, LANES)), bf(_pad_to(p["g2"], 0, LANES))
    row = pl.BlockSpec((tm, d), lambda i: (i, 0))
    full = lambda a: pl.BlockSpec(a.shape, lambda i: (0,) * a.ndim)
    args = (xn, prev, p["mix"], bf(p["wr"]), bf(p["wk"]), bf(p["wv"]), w1, w2, p["w0"].reshape(1, d),
            a1, a2, p["a0"].reshape(1, d), g1, g2)
    return pl.pallas_call(
        _rk_proj_kernel,
        grid=(m // tm,),
        in_specs=[row, row] + [full(a) for a in args[2:]],
        out_specs=[row] * 6,
        out_shape=[jax.ShapeDtypeStruct((m, d), F32)] * 6,
        compiler_params=_cparams(("parallel",)),
        name="rk_proj",
    )(*args)


def _rk_scan_kernel(r_ref, k_ref, v_ref, wl_ref, a_ref, g_ref, kk_ref, ka_ref, rk_ref, lg_ref, lb_ref,
                    s0_ref, z_ref, sT_ref, s_scr, *, chunk, t_valid, npair, passes, state_passes):
    c = pl.program_id(2)
    C = chunk
    C2 = 2 * C

    @pl.when(c == 0)
    def _():
        s_scr[...] = s0_ref[0]

    lane = _iota((C, LANES), 1)
    head0 = lane < RK_HEAD
    same_head = ((_iota((LANES, LANES), 0) < RK_HEAD) == (_iota((LANES, LANES), 1) < RK_HEAD))
    bd_ones = same_head.astype(BF16)
    eye = _iota((LANES, LANES), 0) == _iota((LANES, LANES), 1)
    tri_incl = (_iota((C, C), 0) >= _iota((C, C), 1)).astype(BF16)
    ri = _iota((C2, C2), 0)
    ci = _iota((C2, C2), 1)
    same_blk = (ri < C) == (ci < C)
    strict = same_blk & (ri > ci)
    incl = same_blk & (ri >= ci)
    eye2 = jnp.where(ri == ci, 1.0, 0.0)
    valid = _iota((C, LANES), 0) < t_valid
    dp = functools.partial(_dot, passes=passes)
    ds = functools.partial(_dot, passes=state_passes)
    inv_n = 1.0 / RK_HEAD

    def stack(x):
        return jnp.concatenate([jnp.where(head0, x, 0.0), jnp.where(head0, 0.0, x)], axis=0)

    def fold(x):
        return x[:C] + x[C:]

    def each(fn, *lists):
        return [fn(*xs) for xs in zip(*lists)]

    cat0 = lambda *xs: jnp.concatenate(xs, axis=0)
    cat1 = lambda *xs: jnp.concatenate(xs, axis=1)
    lanes = [slice(pr * LANES, (pr + 1) * LANES) for pr in range(npair)]
    r = [r_ref[0, :, ln] for ln in lanes]
    k = [k_ref[0, :, ln] for ln in lanes]
    v = [v_ref[0, :, ln] for ln in lanes]
    a = [a_ref[0, :, ln] for ln in lanes]
    logw = [-jnp.exp(wl_ref[0, :, ln]) for ln in lanes]
    kk = [ki * kk_ref[:, ln] for ki, ln in zip(k, lanes)]
    k2 = [ki * (1.0 + (ai - 1.0) * ka_ref[:, ln]) for ki, ai, ln in zip(k, a, lanes)]
    sums = [_dot_exact_rhs(cat0(kki * kki, ri_ * k2i * rk_ref[:, ln]), bd_ones)
            for kki, ri_, k2i, ln in zip(kk, r, k2, lanes)]
    kk = each(lambda kki, si: kki / jnp.maximum(jnp.sqrt(si[:C]), 1e-12), kk, sums)
    bonus = each(lambda si, vi: si[C:] * vi, sums, v)
    b = each(lambda kki, ai: kki * ai, kk, a)
    if t_valid < C:
        logw = [jnp.where(valid, x, 0.0) for x in logw]
        b = [jnp.where(valid, x, 0.0) for x in b]
        k2 = [jnp.where(valid, x, 0.0) for x in k2]

    cum = [_dot_exact_lhs(tri_incl, x) for x in logw]
    gam = [jnp.exp(x) for x in cum]
    inv_gam = [jnp.exp(-x) for x in cum]
    gam_prev = each(lambda cu, lw: jnp.exp(cu - lw), cum, logw)
    g_last = [x[C - 1:C, :] for x in gam]

    bt = each(lambda x, ig: x * ig, b, inv_gam)
    kt = each(lambda x, ig: x * ig, k2, inv_gam)
    ats = each(lambda kki, gp: stack(-kki * gp), kk, gam_prev)
    bts, kts, vs = [stack(x) for x in bt], [stack(x) for x in kt], [stack(x) for x in v]
    rts = each(lambda ri_, gi: stack(ri_ * gi), r, gam)
    if C2 % LANES == 0:
        gram = each(lambda at_, rt_, bt_, kt_: dp(cat0(at_, rt_), cat0(bt_, kt_), 1, 1), ats, rts, bts, kts)
        g_ab, g_ak = [x[:C2, :C2] for x in gram], [x[:C2, C2:] for x in gram]
        g_rb, g_rk = [x[C2:, :C2] for x in gram], [x[C2:, C2:] for x in gram]
    else:
        g_ab, g_ak = each(lambda x, y_: dp(x, y_, 1, 1), ats, bts), each(lambda x, y_: dp(x, y_, 1, 1), ats, kts)
        g_rb, g_rk = each(lambda x, y_: dp(x, y_, 1, 1), rts, bts), each(lambda x, y_: dp(x, y_, 1, 1), rts, kts)
    a_ak = [jnp.where(strict, x, 0.0) for x in g_ak]
    m_rb = [jnp.where(incl, x, 0.0) for x in g_rb]
    m_rk = [jnp.where(incl, x, 0.0) for x in g_rk]

    x = [jnp.where(strict, g, 0.0) for g in g_ab]
    t_inv = [eye2 + xi for xi in x]
    for _ in range(int(math.log2(C)) - 1):
        x = [dp(xi, xi) for xi in x]
        t_inv = each(lambda ti, xi: ti + dp(ti, xi), t_inv, x)

    ws = each(dp, a_ak, vs)
    tp = each(lambda ti, at_, wi: dp(ti, cat1(at_, wi)), t_inv, ats, ws)
    mq = each(dp, m_rb, tp)
    mv = each(dp, m_rk, vs)
    q1 = each(lambda rt_, m_: fold(rt_ + m_[:, :LANES]), rts, mq)
    q2 = each(lambda m_, mv_: fold(m_[:, LANES:] + mv_), mq, mv)
    p1 = [fold(t[:, :LANES]) for t in tp]
    p2 = [fold(t[:, LANES:]) for t in tp]

    s = [s_scr[pr] for pr in range(npair)]
    y = each(lambda q1_, s_, q2_: ds(q1_, s_, 1, 1) + q2_, q1, s, q2)
    bp = each(lambda x_, gl: x_ * gl, bt, g_last)
    kp = each(lambda x_, gl: x_ * gl, kt, g_last)
    gm = each(lambda gl, p1_, bp_: jnp.where(eye, gl, 0.0) + jnp.where(same_head, dp(p1_, bp_, 0, 0), 0.0),
              g_last, p1, bp)
    hm = each(lambda p2_, v_, bp_, kp_: jnp.where(same_head, dp(cat0(p2_, v_), cat0(bp_, kp_), 0, 0), 0.0),
              p2, v, bp, kp)
    s_new = each(lambda s_, gm_, hm_: ds(s_, gm_) + hm_, s, gm, hm)
    for pr in range(npair):
        s_scr[pr] = s_new[pr]

    mu = [_dot_exact_rhs(yi, bd_ones) * inv_n for yi in y]
    d = each(lambda yi, mi: yi - mi, y, mu)
    var = [_dot_exact_rhs(di * di, bd_ones) * inv_n for di in d]
    for pr, ln in enumerate(lanes):
        yn = d[pr] * lax.rsqrt(var[pr] + GN_EPS) * lg_ref[:, ln] + lb_ref[:, ln]
        z_ref[0, :, ln] = (yn + bonus[pr]) * g_ref[0, :, ln]

    @pl.when(c == pl.num_programs(2) - 1)
    def _():
        sT_ref[0] = s_scr[...]


def _pair_state(s):
    bsz, h, n, _ = s.shape
    s = s.reshape(bsz, h // 2, 2, n, n)
    z = jnp.zeros_like(s[:, :, 0])
    top = jnp.concatenate([s[:, :, 0], z], axis=-1)
    bot = jnp.concatenate([z, s[:, :, 1]], axis=-1)
    return jnp.concatenate([top, bot], axis=-2)


def _unpair_state(sp):
    n = sp.shape[-1] // 2
    s = jnp.stack([sp[:, :, :n, :n], sp[:, :, n:, n:]], axis=2)
    return s.reshape(sp.shape[0], sp.shape[1] * 2, n, n)


def _rk_scan(r, k, v, wl, a, g, p, s0, chunk, t_valid, npair=8, passes=1, state_passes=3):
    bsz, t, d = r.shape
    width = npair * LANES
    ngroup = d // width
    nchunk = t // chunk
    tok = pl.BlockSpec((1, chunk, width), lambda b, h, c: (b, c, h))
    par = pl.BlockSpec((1, width), lambda b, h, c: (0, h))
    st = pl.BlockSpec((1, npair, LANES, LANES), lambda b, h, c: (b, h, 0, 0))
    vec = lambda x: x.reshape(1, d)
    z, s_t = pl.pallas_call(
        functools.partial(_rk_scan_kernel, chunk=chunk, t_valid=t_valid, npair=npair, passes=passes,
                          state_passes=state_passes),
        grid=(bsz, ngroup, nchunk),
        in_specs=[tok] * 6 + [par] * 5 + [st],
        out_specs=[tok, st],
        out_shape=[jax.ShapeDtypeStruct((bsz, t, d), F32),
                   jax.ShapeDtypeStruct((bsz, d // LANES, LANES, LANES), F32)],
        scratch_shapes=[pltpu.VMEM((npair, LANES, LANES), F32)],
        compiler_params=_cparams(("parallel", "parallel", "arbitrary")),
        name="rk_scan",
    )(r, k, v, wl, a, g, vec(p["k_k"]), vec(p["k_a"]), vec(p["r_k"]), vec(p["lnx_g"]), vec(p["lnx_b"]),
      _pair_state(s0))
    return z, _unpair_state(s_t)


def _mm_res_kernel(x_ref, w_ref, res_ref, o_ref):
    o_ref[...] = res_ref[...] + jnp.dot(x_ref[...].astype(BF16), w_ref[...], preferred_element_type=F32)


def _mm_res(x, w, res, tm):
    m, kd = x.shape
    n = w.shape[1]
    return pl.pallas_call(
        _mm_res_kernel,
        grid=(m // tm,),
        in_specs=[pl.BlockSpec((tm, kd), lambda i: (i, 0)), pl.BlockSpec((kd, n), lambda i: (0, 0)),
                  pl.BlockSpec((tm, n), lambda i: (i, 0))],
        out_specs=pl.BlockSpec((tm, n), lambda i: (i, 0)),
        out_shape=jax.ShapeDtypeStruct((m, n), F32),
        compiler_params=_cparams(("parallel",)),
        name="mm_res",
    )(x, w.astype(BF16), res)


def _ffn_kernel(x_ref, g_ref, wg_ref, wu_ref, wd_ref, o_ref):
    x = x_ref[...]
    xn = _rms(x, g_ref[...]).astype(BF16)
    gt = jnp.dot(xn, wg_ref[...], preferred_element_type=F32)
    up = jnp.dot(xn, wu_ref[...], preferred_element_type=F32)
    h = (jax.nn.silu(gt) * up).astype(BF16)
    o_ref[...] = x + jnp.dot(h, wd_ref[...], preferred_element_type=F32)


def _ffn(x, g, w_gu, w_down, tm):
    m, d = x.shape
    ff = w_down.shape[0]
    wg, wu = w_gu[:, :ff].astype(BF16), w_gu[:, ff:].astype(BF16)
    full = lambda a: pl.BlockSpec(a.shape, lambda i: (0,) * a.ndim)
    row = pl.BlockSpec((tm, d), lambda i: (i, 0))
    wd = w_down.astype(BF16)
    return pl.pallas_call(
        _ffn_kernel,
        grid=(m // tm,),
        in_specs=[row, pl.BlockSpec((1, d), lambda i: (0, 0)), full(wg), full(wu), full(wd)],
        out_specs=row,
        out_shape=jax.ShapeDtypeStruct((m, d), F32),
        compiler_params=_cparams(("parallel",)),
        name="ffn",
    )(x, g.reshape(1, d), wg, wu, wd)


SLOT_BLOCK = 128


def _moe_route_kernel(x_ref, g_ref, rt_ref, xn_ref, pos_ref, post_ref, gate_ref, cnt_ref):
    tm = x_ref.shape[0]
    xn = _rms(x_ref[...], g_ref[...])
    xn_ref[...] = xn.astype(BF16)
    logits = _dot(xn, rt_ref[...], passes=6)
    lt = logits.T[:N_EXPERTS]
    row = _iota(lt.shape, 0)
    v1 = jnp.max(lt, axis=0, keepdims=True)
    i1 = jnp.min(jnp.where(lt == v1, row, N_EXPERTS), axis=0, keepdims=True)
    rest = jnp.where(row == i1, NEG_INF, lt)
    v2 = jnp.max(rest, axis=0, keepdims=True)
    i2 = jnp.min(jnp.where(rest == v2, row, N_EXPERTS), axis=0, keepdims=True)
    e2 = jnp.exp(v2 - v1)
    w1 = 1.0 / (1.0 + e2)
    w2 = e2 / (1.0 + e2)
    sel1, sel2 = row == i1, row == i2
    mask = sel1 | sel2
    gate_ref[0] = jnp.where(sel1, w1, 0.0) + jnp.where(sel2, w2, 0.0)
    before = (_iota((tm, tm), 0) < _iota((tm, tm), 1)).astype(BF16)
    maskf = jnp.where(mask, 1.0, 0.0)
    rank = _dg(maskf.astype(BF16), before, 1, 0)
    pos = jnp.where(mask, rank, -1.0)
    pos_ref[0] = pos
    filler = jnp.full((LANES - N_EXPERTS, tm), -1.0, F32)
    post_ref[...] = jnp.concatenate([pos, filler], axis=0).T
    cnt = jnp.sum(maskf, axis=1, keepdims=True)
    cnt_ref[0] = jnp.broadcast_to(cnt, (N_EXPERTS, LANES)).astype(jnp.int32)


def _moe_expert_kernel(cnt_ref, x_ref, xn_ref, pos_ref, post_ref, gate_ref, wg_ref, wu_ref, wd_ref, o_ref):
    i = pl.program_id(0)
    e = pl.program_id(1)
    tm = x_ref.shape[0]

    @pl.when(e == 0)
    def _():
        o_ref[...] = x_ref[...]

    pos_row = pos_ref[0, pl.ds(e, 1), :]
    gate_row = gate_ref[0, pl.ds(e, 1), :]
    post = post_ref[...]
    pos_col = jnp.sum(jnp.where(_iota(post.shape, 1) == e, post, 0.0), axis=1, keepdims=True)

    def block(blk, carry):
        base = (blk * SLOT_BLOCK).astype(F32)
        pick = pos_row == base + _iota((SLOT_BLOCK, tm), 0).astype(F32)
        xe = jnp.dot(pick.astype(BF16), xn_ref[...], preferred_element_type=F32).astype(BF16)
        gt = jnp.dot(xe, wg_ref[0], preferred_element_type=F32)
        up = jnp.dot(xe, wu_ref[0], preferred_element_type=F32)
        h = (jax.nn.silu(gt) * up).astype(BF16)
        ye = jnp.dot(h, wd_ref[0], preferred_element_type=F32)
        gslot = jnp.sum(jnp.where(pick, gate_row, 0.0), axis=1, keepdims=True)
        put = pos_col == base + _iota((tm, SLOT_BLOCK), 1).astype(F32)
        o_ref[...] += jnp.dot(put.astype(BF16), (ye * gslot).astype(BF16), preferred_element_type=F32)
        return carry

    lax.fori_loop(0, (cnt_ref[i, e] + (SLOT_BLOCK - 1)) // SLOT_BLOCK, block, 0)


def _moe(x, g, router, w_gu, w_down, tm):
    m, d = x.shape
    ne, ffe = w_down.shape[0], w_down.shape[1]
    nt = m // tm
    wg, wu = w_gu[:, :, :ffe].astype(BF16), w_gu[:, :, ffe:].astype(BF16)
    wd = w_down.astype(BF16)
    rt = _pad_to(router, 1, LANES)
    f = jax.ShapeDtypeStruct
    row1 = pl.BlockSpec((tm, d), lambda i: (i, 0))
    xn, pos, post, gate, cnt = pl.pallas_call(
        _moe_route_kernel,
        grid=(nt,),
        in_specs=[row1, pl.BlockSpec((1, d), lambda i: (0, 0)), pl.BlockSpec(rt.shape, lambda i: (0, 0))],
        out_specs=[row1, pl.BlockSpec((1, ne, tm), lambda i: (i, 0, 0)),
                   pl.BlockSpec((tm, LANES), lambda i: (i, 0)), pl.BlockSpec((1, ne, tm), lambda i: (i, 0, 0)),
                   pl.BlockSpec((1, ne, LANES), lambda i: (i, 0, 0))],
        out_shape=[f((m, d), BF16), f((nt, ne, tm), F32), f((m, LANES), F32), f((nt, ne, tm), F32),
                   f((nt, ne, LANES), jnp.int32)],
        compiler_params=_cparams(("parallel",)),
        name="moe_route",
    )(x, g.reshape(1, d), rt)
    row = pl.BlockSpec((tm, d), lambda i, e, c: (i, 0))
    per_tile = pl.BlockSpec((1, ne, tm), lambda i, e, c: (i, 0, 0))
    grid_spec = pltpu.PrefetchScalarGridSpec(
        num_scalar_prefetch=1,
        grid=(nt, ne),
        in_specs=[row, row, per_tile, pl.BlockSpec((tm, LANES), lambda i, e, c: (i, 0)), per_tile,
                  pl.BlockSpec((1, d, ffe), lambda i, e, c: (e, 0, 0)),
                  pl.BlockSpec((1, d, ffe), lambda i, e, c: (e, 0, 0)),
                  pl.BlockSpec((1, ffe, d), lambda i, e, c: (e, 0, 0))],
        out_specs=row,
    )
    return pl.pallas_call(
        _moe_expert_kernel,
        grid_spec=grid_spec,
        out_shape=f((m, d), F32),
        compiler_params=_cparams(("parallel", "arbitrary")),
        name="moe",
    )(cnt[:, :, 0], x, xn, pos, post, gate, wg, wu, wd)


def _rope_tables(pos):
    inv = ROPE_THETA ** (-jnp.arange(0, ROT_DIM, 2, dtype=F32) / ROT_DIM)
    ang = pos.astype(F32)[:, None] * inv[None, :]
    cos, sin = jnp.cos(ang), jnp.sin(ang)
    ones = jnp.ones((pos.shape[0], AT_HEAD_DIM - ROT_DIM), F32)
    cos_g = jnp.concatenate([cos, cos, ones], axis=1)
    sin_g = jnp.concatenate([-sin, sin, 0.0 * ones], axis=1)
    return jnp.tile(cos_g, (1, 2)), jnp.tile(sin_g, (1, 2))


def _norm_rope(x, gain, cos, sin):
    lane = _iota(x.shape, 1)
    first = lane < AT_HEAD_DIM
    sq = x * x
    s0 = jnp.sum(jnp.where(first, sq, 0.0), axis=-1, keepdims=True)
    s1 = jnp.sum(jnp.where(first, 0.0, sq), axis=-1, keepdims=True)
    ms = jnp.where(first, s0, s1) * (1.0 / AT_HEAD_DIM)
    y = x * lax.rsqrt(ms + RMS_EPS) * gain
    half = ROT_DIM // 2
    in_lo = (lane % AT_HEAD_DIM) < half
    partner = jnp.where(in_lo, pltpu.roll(y, LANES - half, axis=1), pltpu.roll(y, half, axis=1))
    return y * cos + partner * sin


def _qkv_kernel(x_ref, g_ref, w_ref, qg_ref, kg_ref, cos_ref, sin_ref, *outs, transpose_k):
    j = pl.program_id(1)
    xn_scr = outs[-1]
    nh = x_ref.shape[1] // LANES

    @pl.when(j == 0)
    def _():
        xn_scr[...] = _rms(x_ref[...], g_ref[...]).astype(BF16)

    y = jnp.dot(xn_scr[...], w_ref[...], preferred_element_type=F32)
    heads = [slice(h * LANES, (h + 1) * LANES) for h in range(nh)]

    @pl.when(j == 0)
    def _():
        for hs in heads:
            outs[0][:, hs] = _norm_rope(y[:, hs], qg_ref[...], cos_ref[...], sin_ref[...])

    @pl.when(j == 1)
    def _():
        for hs in heads:
            kh = _norm_rope(y[:, hs], kg_ref[...], cos_ref[...], sin_ref[...])
            if transpose_k:
                kt = kh.T
                outs[1][0, hs, :] = kt
                outs[2][0, 0, hs, :] = kt.astype(BF16)
            else:
                outs[1][:, hs] = kh

    @pl.when(j == 2)
    def _():
        if transpose_k:
            outs[3][...] = y
            outs[4][...] = y.astype(BF16)
        else:
            outs[2][...] = y


def _qkv_proj(x, g, w_in, qn_g, kn_g, cos_t, sin_t, tm, seq, transpose_k):
    m, d = x.shape
    nt = cos_t.shape[0] // tm
    row = pl.BlockSpec((tm, d), lambda i, j: (i, 0))
    vec = pl.BlockSpec((1, LANES), lambda i, j: (0, 0))
    tab = pl.BlockSpec((tm, LANES), lambda i, j: (i % nt, 0))
    f = jax.ShapeDtypeStruct
    if transpose_k:
        bsz, ns = m // seq, seq // tm
        out_shape = [f((m, d), F32), f((bsz, d, seq), F32), f((bsz, ns, d, tm), BF16), f((m, d), F32),
                     f((m, d), BF16)]
        out_specs = [row, pl.BlockSpec((1, d, tm), lambda i, j: (i // ns, 0, i % ns)),
                     pl.BlockSpec((1, 1, d, tm), lambda i, j: (i // ns, i % ns, 0, 0)), row, row]
    else:
        out_shape = [f((m, d), F32)] * 3
        out_specs = [row] * 3
    return pl.pallas_call(
        functools.partial(_qkv_kernel, transpose_k=transpose_k),
        grid=(m // tm, 3),
        in_specs=[row, pl.BlockSpec((1, d), lambda i, j: (0, 0)), pl.BlockSpec((d, d), lambda i, j: (0, j)),
                  vec, vec, tab, tab],
        out_specs=out_specs,
        out_shape=out_shape,
        scratch_shapes=[pltpu.VMEM((tm, d), BF16)],
        compiler_params=_cparams(("parallel", "arbitrary")),
        name="qkv_proj",
    )(x, g.reshape(1, d), w_in.astype(BF16), jnp.tile(qn_g, 2).reshape(1, LANES),
      jnp.tile(kn_g, 2).reshape(1, LANES), cos_t, sin_t)


def _diff_finish(o1, o2, lam, sub_g, lam_init):
    o = o1 - lam * o2
    o = o * lax.rsqrt(jnp.mean(o * o, axis=-1, keepdims=True) + RMS_EPS) * sub_g
    return o * (1.0 - lam_init)


def _branch_rows(q):
    first = _iota(q.shape, 1) < AT_HEAD_DIM
    return jnp.concatenate([jnp.where(first, q, 0.0), jnp.where(first, 0.0, q)], axis=0)


def _attn_prompt_kernel(lam_ref, q_ref, kt_ref, v_ref, sg_ref, o_ref, m_ref, l_ref, acc_ref,
                        *, tq, tk, lam_init):
    qi = pl.program_id(2)
    qs = _branch_rows(q_ref[0] * (ATTN_SCALE * LOG2E)).astype(BF16)
    m_ref[...] = jnp.full(m_ref.shape, NEG_INF, F32)
    l_ref[...] = jnp.zeros(l_ref.shape, F32)
    acc_ref[...] = jnp.zeros(acc_ref.shape, F32)

    def tile(j, masked):
        s = _dg(qs, kt_ref[0, j], 1, 0)
        if masked:
            qpos = qi * tq + (_iota(s.shape, 0) % tq)
            kpos = j * tk + _iota(s.shape, 1)
            s = jnp.where(kpos <= qpos, s, NEG_INF)
        m_prev = m_ref[...]
        m_new = jnp.maximum(m_prev, jnp.max(s, axis=-1, keepdims=True))
        alpha = jnp.exp2(m_prev - m_new)
        p = jnp.exp2(s - jnp.concatenate([m_new] * (tk // LANES), axis=1))
        l_ref[...] = alpha * l_ref[...] + jnp.sum(p, axis=-1, keepdims=True)
        acc_ref[...] = alpha * acc_ref[...] + jnp.dot(p.astype(BF16), v_ref[0, j],
                                                      preferred_element_type=F32)
        m_ref[...] = m_new

    j_last = (qi * tq + (tq - 1)) // tk

    def body(jj, carry):
        tile(2 * jj, False)
        tile(2 * jj + 1, False)
        return carry

    lax.fori_loop(0, j_last // 2, body, 0)

    @pl.when(j_last % 2 == 1)
    def _():
        tile(j_last - 1, False)

    tile(j_last, True)
    o = acc_ref[...] / l_ref[...]
    o_ref[0] = _diff_finish(o[:tq], o[tq:], lam_ref[0], sg_ref[...], lam_init)


def _attn_prompt(q, kt_bf, v_bf, lam, sub_g, lam_init, bsz, seq, tq):
    d = q.shape[1]
    nh = d // LANES
    nk, tk = kt_bf.shape[1], kt_bf.shape[3]
    out = pl.pallas_call(
        functools.partial(_attn_prompt_kernel, tq=tq, tk=tk, lam_init=lam_init),
        grid=(bsz, nh, seq // tq),
        in_specs=[pl.BlockSpec(memory_space=pltpu.SMEM),
                  pl.BlockSpec((1, tq, LANES), lambda b, h, i: (b, i, h)),
                  pl.BlockSpec((1, nk, LANES, tk), lambda b, h, i: (b, 0, h, 0)),
                  pl.BlockSpec((1, nk, tk, LANES), lambda b, h, i: (b, 0, 0, h)),
                  pl.BlockSpec((1, LANES), lambda b, h, i: (0, 0))],
        out_specs=pl.BlockSpec((1, tq, LANES), lambda b, h, i: (b, i, h)),
        out_shape=jax.ShapeDtypeStruct((bsz, seq, d), F32),
        scratch_shapes=[pltpu.VMEM((2 * tq, LANES), F32), pltpu.VMEM((2 * tq, LANES), F32),
                        pltpu.VMEM((2 * tq, LANES), F32)],
        compiler_params=_cparams(("parallel", "parallel", "arbitrary")),
        name="attn_prompt",
    )(lam.reshape(1), q.reshape(bsz, seq, d), kt_bf, v_bf.reshape(bsz, nk, tk, d), sub_g.reshape(1, LANES))
    return out.reshape(bsz * seq, d)


def _attn_decode_kernel(pt_ref, lam_ref, q_ref, kn_ref, vn_ref, sg_ref, *rest, nq, nh, pp, lam_init):
    k_refs, v_refs = rest[:pp], rest[pp:2 * pp]
    o_ref, m_ref, l_ref, acc_ref = rest[2 * pp:]
    step = pl.program_id(1)
    rows = q_ref.shape[1] // nh
    rpb = rows // 2
    page = k_refs[0].shape[3]

    @pl.when(step == 0)
    def _():
        m_ref[...] = jnp.full(m_ref.shape, NEG_INF, F32)
        l_ref[...] = jnp.zeros(l_ref.shape, F32)
        acc_ref[...] = jnp.zeros(acc_ref.shape, F32)

    q = q_ref[0] * ATTN_SCALE
    keep = ((_iota(q.shape, 0) % rows) < rpb) == (_iota(q.shape, 1) < AT_HEAD_DIM)
    qs = jnp.where(keep, q, 0.0)
    hrows = [slice(h * rows, (h + 1) * rows) for h in range(nh)]
    bf = lambda x: x.astype(BF16)

    def update(s, pv_fn):
        m_prev = m_ref[...]
        m_new = jnp.maximum(m_prev, jnp.max(s, axis=-1, keepdims=True))
        alpha = jnp.exp(m_prev - m_new)
        p = jnp.exp(s - m_new)
        l_ref[...] = alpha * l_ref[...] + jnp.sum(p, axis=-1, keepdims=True)
        acc_ref[...] = alpha * acc_ref[...] + pv_fn(p)
        m_ref[...] = m_new

    s = jnp.concatenate(
        [jnp.concatenate([_dg(bf(qs[hr]), bf(k_refs[i][0, 0, h * LANES:(h + 1) * LANES, :]), 1, 0)
                          for h, hr in enumerate(hrows)], axis=0) for i in range(pp)], axis=1)

    def pv_pages(p):
        outs = []
        for h, hr in enumerate(hrows):
            o = None
            for i in range(pp):
                vh = bf(v_refs[i][0, 0, pl.ds(h, page, stride=nh), :])
                t = jnp.dot(bf(p[hr, i * page:(i + 1) * page]), vh, preferred_element_type=F32)
                o = t if o is None else o + t
            outs.append(o)
        return jnp.concatenate(outs, axis=0)

    update(s, pv_pages)

    @pl.when(step == pl.num_programs(1) - 1)
    def _():
        nk = kn_ref.shape[2]
        sn = jnp.concatenate([_dg(bf(qs[hr]), bf(kn_ref[0, h]), 1, 1) for h, hr in enumerate(hrows)],
                             axis=0)
        col = _iota(sn.shape, 1)
        causal = (col <= (_iota(sn.shape, 0) % rpb)) & (col < nq)
        update(jnp.where(causal, sn, NEG_INF),
               lambda p: jnp.concatenate([jnp.dot(bf(p[hr]), bf(vn_ref[0, h]), preferred_element_type=F32)
                                          for h, hr in enumerate(hrows)], axis=0))
        o = acc_ref[...] / l_ref[...]
        total = o.shape[0]
        o_ref[0] = _diff_finish(o, pltpu.roll(o, total - rpb, axis=0), lam_ref[0], sg_ref[...], lam_init)


def _attn_decode(q, k, v, cache_k, cache_v, layer, page_table, lam, sub_g, lam_init, bsz, nq, pp):
    d = q.shape[1]
    nh = d // LANES
    n_pages = page_table.shape[1]
    n_layers, n_pool, page = cache_k.shape[:3]
    kt = jnp.transpose(cache_k, (0, 1, 3, 4, 5, 2)).reshape(n_layers, n_pool, d, page)
    vv = cache_v.reshape(n_layers, n_pool, page * nh, LANES)
    rpb = -(-nq // 4) * 4
    heads_first = lambda x: jnp.transpose(x.reshape(bsz, nq, nh, LANES), (0, 2, 1, 3))
    qh = _pad_to(heads_first(q), 2, rpb)
    qh = jnp.concatenate([qh, qh], axis=2).reshape(bsz, nh * 2 * rpb, LANES)
    kh = _pad_to(heads_first(k), 2, 16)
    vh = _pad_to(heads_first(v), 2, 16)
    rows = nh * 2 * rpb
    qspec = pl.BlockSpec((1, rows, LANES), lambda b, s, pt: (b, 0, 0))
    nspec = pl.BlockSpec((1, nh, kh.shape[2], LANES), lambda b, s, pt: (b, 0, 0, 0))

    def page_spec(i, nrows):
        return pl.BlockSpec((1, 1, nrows, LANES), lambda b, s, pt: (layer, pt[b, s * pp + i], 0, 0))

    grid_spec = pltpu.PrefetchScalarGridSpec(
        num_scalar_prefetch=1,
        grid=(bsz, n_pages // pp),
        in_specs=([pl.BlockSpec(memory_space=pltpu.SMEM), qspec, nspec, nspec,
                   pl.BlockSpec((1, LANES), lambda b, s, pt: (0, 0))]
                  + [page_spec(i, d) for i in range(pp)] + [page_spec(i, page * nh) for i in range(pp)]),
        out_specs=qspec,
        scratch_shapes=[pltpu.VMEM((rows, 1), F32), pltpu.VMEM((rows, 1), F32),
                        pltpu.VMEM((rows, LANES), F32)],
    )
    out = pl.pallas_call(
        functools.partial(_attn_decode_kernel, nq=nq, nh=nh, pp=pp, lam_init=lam_init),
        grid_spec=grid_spec,
        out_shape=jax.ShapeDtypeStruct((bsz, rows, LANES), F32),
        compiler_params=_cparams(("parallel", "arbitrary")),
        name="attn_decode",
    )(page_table, lam.reshape(1), qh, kh, vh, sub_g.reshape(1, LANES), *([kt] * pp), *([vv] * pp))
    out = out.reshape(bsz, nh, 2, rpb, LANES)[:, :, 0, :nq]
    return jnp.transpose(out, (0, 2, 1, 3)).reshape(bsz * nq, d)


def _row_tile(m, want):
    t = min(m, want)
    while m % t:
        t //= 2
    return t


def _rwkv_layer(x, shift_prev, s0, norm_g, p):
    bsz, t, d = x.shape
    m = bsz * t
    xn = _rmsnorm_rows(x.reshape(m, d), norm_g, _row_tile(m, 512)).reshape(bsz, t, d)
    prev = jnp.concatenate([shift_prev[:, None, :], xn[:, :-1]], axis=1)
    r, k, v, wl, a, g = _rk_proj(xn.reshape(m, d), prev.reshape(m, d), p, _row_tile(m, 256))
    chunk = 64 if t % 64 == 0 else 8
    t_pad = -(-t // chunk) * chunk
    seqs = [u.reshape(bsz, t, d) for u in (r, k, v, wl, a, g)]
    if t_pad != t:
        seqs = [_pad_to(u, 1, chunk) for u in seqs]
    z, s_t = _rk_scan(*seqs, p, s0, chunk, t if t_pad != t else chunk)
    z = z[:, :t].reshape(m, d)
    out = _mm_res(z, p["wo"], x.reshape(m, d), _row_tile(m, 512))
    return out.reshape(bsz, t, d), xn[:, -1], s_t


def _attn_layer(x, pos, norm_g, w_in, qn_g, kn_g, lam, lam_init, sub_g, wo, decode=None):
    bsz, t, d = x.shape
    m = bsz * t
    nh = d // LANES
    x2 = x.reshape(m, d)
    cos_t, sin_t = _rope_tables(pos)
    if decode is None:
        tm = _row_tile(t, 512)
        q, k_t, kt_bf, v, v_bf = _qkv_proj(x2, norm_g, w_in, qn_g, kn_g, cos_t, sin_t, tm, t, True)
        o = _attn_prompt(q, kt_bf, v_bf, lam, sub_g, lam_init, bsz, t, _row_tile(t, 256))
        k_out = jnp.transpose(k_t.reshape(bsz, nh, 2, AT_HEAD_DIM, t), (0, 4, 1, 2, 3))
    else:
        cache_k, cache_v, layer, page_table = decode
        cos_t, sin_t = jnp.tile(cos_t, (bsz, 1)), jnp.tile(sin_t, (bsz, 1))
        q, k, v = _qkv_proj(x2, norm_g, w_in, qn_g, kn_g, cos_t, sin_t, m, t, False)
        pp = _row_tile(page_table.shape[1], 8)
        o = _attn_decode(q, k, v, cache_k, cache_v, layer, page_table, lam, sub_g, lam_init, bsz, t, pp)
        k_out = k.reshape(bsz, t, nh, 2, AT_HEAD_DIM)
    y = _mm_res(o, wo, x2, _row_tile(m, 512)).reshape(bsz, t, d)
    return y, k_out, v.reshape(bsz, t, nh, 2 * AT_HEAD_DIM)


def kernel(x_prompt, x_sample, state_wkv, state_shift, cache_k, cache_v, page_table,
           rk_norm_g, rk_mix, rk_wr, rk_wk, rk_wv, rk_w0, rk_w1, rk_w2, rk_a0, rk_a1, rk_a2,
           rk_g1, rk_g2, rk_k_k, rk_k_a, rk_r_k, rk_lnx_g, rk_lnx_b, rk_wo,
           ffn_norm_g, ffn_w_gu, ffn_w_down,
           at_norm_g, at_w_in, at_q_norm_g, at_k_norm_g, at_lq1, at_lk1, at_lq2, at_lk2,
           at_subln_g, at_wo,
           moe_norm_g, moe_router, moe_w_gu, moe_w_down):
    yp, ys = x_prompt, x_sample
    bp, tp, d = x_prompt.shape
    bs, ts, _ = x_sample.shape
    depth = rk_norm_g.shape[0] + at_norm_g.shape[0]
    past = page_table.shape[1] * cache_k.shape[2]
    pos_p = jnp.arange(tp)
    pos_s = past + jnp.arange(ts)
    nheads = d // RK_HEAD
    p_wkv, p_shift, p_k, p_v = [], [], [], []
    s_wkv, s_shift, s_k, s_v = [], [], [], []
    for i in range(depth):
        j = i // 2
        if i % 2 == 0:
            p = dict(mix=rk_mix[j], wr=rk_wr[j], wk=rk_wk[j], wv=rk_wv[j], w0=rk_w0[j], w1=rk_w1[j],
                     w2=rk_w2[j], a0=rk_a0[j], a1=rk_a1[j], a2=rk_a2[j], g1=rk_g1[j], g2=rk_g2[j],
                     k_k=rk_k_k[j], k_a=rk_k_a[j], r_k=rk_r_k[j].reshape(-1), lnx_g=rk_lnx_g[j],
                     lnx_b=rk_lnx_b[j], wo=rk_wo[j])
            yp, sh, st = _rwkv_layer(yp, jnp.zeros((bp, d), F32),
                                     jnp.zeros((bp, nheads, RK_HEAD, RK_HEAD), F32), rk_norm_g[j], p)
            p_shift.append(sh)
            p_wkv.append(st)
            ys, sh, st = _rwkv_layer(ys, state_shift[j], state_wkv[j], rk_norm_g[j], p)
            s_shift.append(sh)
            s_wkv.append(st)
            yp = _ffn(yp.reshape(bp * tp, d), ffn_norm_g[j], ffn_w_gu[j], ffn_w_down[j],
                      _row_tile(bp * tp, 512)).reshape(bp, tp, d)
            ys = _ffn(ys.reshape(bs * ts, d), ffn_norm_g[j], ffn_w_gu[j], ffn_w_down[j],
                      _row_tile(bs * ts, 512)).reshape(bs, ts, d)
        else:
            lam_init = 0.8 - 0.6 * math.exp(-0.3 * i)
            lam = (jnp.exp(jnp.sum(at_lq1[j] * at_lk1[j])) - jnp.exp(jnp.sum(at_lq2[j] * at_lk2[j]))
                   + lam_init)
            common = (at_norm_g[j], at_w_in[j], at_q_norm_g[j], at_k_norm_g[j], lam, lam_init,
                      at_subln_g[j], at_wo[j])
            yp, k_new, v_new = _attn_layer(yp, pos_p, *common)
            p_k.append(k_new)
            p_v.append(v_new)
            ys, k_new, v_new = _attn_layer(ys, pos_s, *common, decode=(cache_k, cache_v, j, page_table))
            s_k.append(k_new)
            s_v.append(v_new)
            yp = _moe(yp.reshape(bp * tp, d), moe_norm_g[j], moe_router[j], moe_w_gu[j], moe_w_down[j],
                      _row_tile(bp * tp, 1024)).reshape(bp, tp, d)
            ys = _moe(ys.reshape(bs * ts, d), moe_norm_g[j], moe_router[j], moe_w_gu[j], moe_w_down[j],
                      _row_tile(bs * ts, 1024)).reshape(bs, ts, d)
    return (yp, ys, jnp.stack(p_wkv), jnp.stack(p_shift), jnp.stack(p_k), jnp.stack(p_v),
            jnp.stack(s_wkv), jnp.stack(s_shift), jnp.stack(s_k), jnp.stack(s_v))
```

```python
import functools
import math

import jax
import jax.numpy as jnp
from jax import lax
from jax.experimental import pallas as pl
from jax.experimental.pallas import tpu as pltpu

F32 = jnp.float32
BF16 = jnp.bfloat16

RK_HEAD = 64
GN_EPS = 64e-5
AT_HEAD_DIM = 64
ROT_DIM = AT_HEAD_DIM // 4
ROPE_THETA = 500000.0
ATTN_SCALE = AT_HEAD_DIM ** -0.5
LOG2E = 1.4426950408889634
NEG_INF = -1e30
N_EXPERTS = 8
RMS_EPS = 1e-6
LANES = 128
VMEM_LIMIT = 56 * 1024 * 1024


def _cparams(sem):
    return pltpu.CompilerParams(dimension_semantics=sem, vmem_limit_bytes=VMEM_LIMIT)


def _dg(a, b, ca, cb):
    return lax.dot_general(a, b, (((ca,), (cb,)), ((), ())), preferred_element_type=F32)


def _split(x, n):
    parts = []
    for i in range(n):
        p = x.astype(BF16)
        parts.append(p)
        if i + 1 < n:
            x = x - p.astype(F32)
    return parts


def _dot(a, b, ca=1, cb=0, passes=3):
    if passes == 1:
        return _dg(a.astype(BF16), b.astype(BF16), ca, cb)
    if passes == 3:
        ah, al = _split(a, 2)
        bh, bl = _split(b, 2)
        return _dg(ah, bh, ca, cb) + (_dg(ah, bl, ca, cb) + _dg(al, bh, ca, cb))
    ah, am, al = _split(a, 3)
    bh, bm, bl = _split(b, 3)
    return (_dg(ah, bh, ca, cb) + (_dg(ah, bm, ca, cb) + _dg(am, bh, ca, cb))
            + (_dg(am, bm, ca, cb) + _dg(ah, bl, ca, cb) + _dg(al, bh, ca, cb)))


def _dot_exact_rhs(a, b_bf16, ca=1, cb=0):
    ah, al = _split(a, 2)
    return _dg(ah, b_bf16, ca, cb) + _dg(al, b_bf16, ca, cb)


def _dot_exact_lhs(a_bf16, b, ca=1, cb=0):
    bh, bl = _split(b, 2)
    return _dg(a_bf16, bh, ca, cb) + _dg(a_bf16, bl, ca, cb)


def _iota(shape, dim):
    return lax.broadcasted_iota(jnp.int32, shape, dim)


def _rms(x, g):
    return x * lax.rsqrt(jnp.mean(x * x, axis=-1, keepdims=True) + RMS_EPS) * g


def _rmsnorm_kernel(x_ref, g_ref, o_ref):
    o_ref[...] = _rms(x_ref[...], g_ref[...])


def _rmsnorm_rows(x, g, tm):
    m, d = x.shape
    return pl.pallas_call(
        _rmsnorm_kernel,
        grid=(m // tm,),
        in_specs=[pl.BlockSpec((tm, d), lambda i: (i, 0)), pl.BlockSpec((1, d), lambda i: (0, 0))],
        out_specs=pl.BlockSpec((tm, d), lambda i: (i, 0)),
        out_shape=jax.ShapeDtypeStruct((m, d), F32),
        compiler_params=_cparams(("parallel",)),
        name="rmsnorm",
    )(x, g.reshape(1, d))


def _rk_proj_kernel(xn_ref, prev_ref, *refs):
    _rk_proj_body(xn_ref[...], prev_ref[...], *refs)


def _rk_proj_fused_kernel(x_ref, halo_ref, shift_ref, ng_ref, *refs, tiles_per_seq):
    i = pl.program_id(0)
    ng = ng_ref[...]
    xn = _rms(x_ref[...], ng)
    halo = halo_ref[0]
    before = _rms(halo[halo.shape[0] - 1:, :], ng)
    first_prev = jnp.where(i % tiles_per_seq == 0, shift_ref[0], before)
    prev = jnp.where(_iota(xn.shape, 0) == 0, first_prev, pltpu.roll(xn, 1, axis=0))
    tail = refs[-1]
    tail[0] = xn[xn.shape[0] - tail.shape[1]:, :]
    _rk_proj_body(xn, prev, *refs[:-1])


def _rk_proj_body(xn, prev, mix_ref, wr_ref, wk_ref, wv_ref, w1_ref, w2_ref, w0_ref,
                  a1_ref, a2_ref, a0_ref, g1_ref, g2_ref,
                  r_ref, k_ref, v_ref, wl_ref, a_ref, g_ref):
    dx = prev - xn
    mix = mix_ref[...]

    def mixed(c):
        return (xn + dx * mix[c:c + 1, :]).astype(BF16)

    def mm(x, w_ref):
        return jnp.dot(x, w_ref[...], preferred_element_type=F32)

    r_ref[...] = mm(mixed(0), wr_ref)
    k_ref[...] = mm(mixed(2), wk_ref)
    v_ref[...] = mm(mixed(3), wv_ref)
    z = w0_ref[...] + mm(jnp.tanh(mm(mixed(1), w1_ref)).astype(BF16), w2_ref)
    wl_ref[...] = -jax.nn.softplus(-z) - 0.5
    a_ref[...] = jax.nn.sigmoid(a0_ref[...] + mm(mm(mixed(4), a1_ref).astype(BF16), a2_ref))
    g_ref[...] = mm(jax.nn.sigmoid(mm(mixed(5), g1_ref)).astype(BF16), g2_ref)


def _pad_to(x, axis, mult):
    pad = (-x.shape[axis]) % mult
    if pad == 0:
        return x
    widths = [(0, 0)] * x.ndim
    widths[axis] = (0, pad)
    return jnp.pad(x, widths)


def _rk_proj_weights(p, d):
    bf = lambda w: w.astype(BF16)
    w1, w2 = bf(_pad_to(p["w1"], 1, LANES)), bf(_pad_to(p["w2"], 0, LANES))
    a1, a2 = bf(_pad_to(p["a1"], 1, LANES)), bf(_pad_to(p["a2"], 0, LANES))
    g1, g2 = bf(_pad_to(p["g1"], 1, LANES)), bf(_pad_to(p["g2"], 0, LANES))
    return (p["mix"], bf(p["wr"]), bf(p["wk"]), bf(p["wv"]), w1, w2, p["w0"].reshape(1, d),
            a1, a2, p["a0"].reshape(1, d), g1, g2)


def _rk_proj(xn, prev, p, tm):
    m, d = xn.shape
    row = pl.BlockSpec((tm, d), lambda i: (i, 0))
    full = lambda a: pl.BlockSpec(a.shape, lambda i: (0,) * a.ndim)
    weights = _rk_proj_weights(p, d)
    return pl.pallas_call(
        _rk_proj_kernel,
        grid=(m // tm,),
        in_specs=[row, row] + [full(a) for a in weights],
        out_specs=[row] * 6,
        out_shape=[jax.ShapeDtypeStruct((m, d), F32)] * 6,
        compiler_params=_cparams(("parallel",)),
        name="rk_proj",
    )(xn, prev, *weights)


SUBLANES = 8


def _rk_proj_fused(x, shift_prev, norm_g, p, tm):
    bsz, t, d = x.shape
    m = bsz * t
    nt = t // tm
    row = pl.BlockSpec((tm, d), lambda i: (i, 0))
    full = lambda a: pl.BlockSpec(a.shape, lambda i: (0,) * a.ndim)
    weights = _rk_proj_weights(p, d)
    halo = pl.BlockSpec((1, SUBLANES, d), lambda i: (jnp.maximum(i * (tm // SUBLANES) - 1, 0), 0, 0))
    outs = pl.pallas_call(
        functools.partial(_rk_proj_fused_kernel, tiles_per_seq=nt),
        grid=(m // tm,),
        in_specs=[row, halo, pl.BlockSpec((1, 1, d), lambda i: (i // nt, 0, 0)),
                  pl.BlockSpec((1, d), lambda i: (0, 0))] + [full(a) for a in weights],
        out_specs=[row] * 6 + [pl.BlockSpec((1, SUBLANES, d), lambda i: (i // nt, 0, 0))],
        out_shape=[jax.ShapeDtypeStruct((m, d), F32)] * 6 + [jax.ShapeDtypeStruct((bsz, SUBLANES, d), F32)],
        compiler_params=_cparams(("arbitrary",)),
        name="rk_proj_fused",
    )(x.reshape(m, d), x.reshape(m // SUBLANES, SUBLANES, d), shift_prev.reshape(bsz, 1, d),
      norm_g.reshape(1, d), *weights)
    return outs[:6], outs[6][:, SUBLANES - 1]


def _rk_scan_kernel(r_ref, k_ref, v_ref, wl_ref, a_ref, g_ref, kk_ref, ka_ref, rk_ref, lg_ref, lb_ref,
                    s0_ref, z_ref, sT_ref, s_scr, *, chunk, t_valid, npair, passes, state_passes):
    c = pl.program_id(2)
    C = chunk
    C2 = 2 * C

    @pl.when(c == 0)
    def _():
        s_scr[...] = s0_ref[0]

    lane = _iota((C, LANES), 1)
    head0 = lane < RK_HEAD
    same_head = ((_iota((LANES, LANES), 0) < RK_HEAD) == (_iota((LANES, LANES), 1) < RK_HEAD))
    bd_ones = same_head.astype(BF16)
    eye = _iota((LANES, LANES), 0) == _iota((LANES, LANES), 1)
    tri_incl = (_iota((C, C), 0) >= _iota((C, C), 1)).astype(BF16)
    ri = _iota((C2, C2), 0)
    ci = _iota((C2, C2), 1)
    same_blk = (ri < C) == (ci < C)
    strict = same_blk & (ri > ci)
    incl = same_blk & (ri >= ci)
    eye2 = jnp.where(ri == ci, 1.0, 0.0)
    valid = _iota((C, LANES), 0) < t_valid
    dp = functools.partial(_dot, passes=passes)
    ds = functools.partial(_dot, passes=state_passes)
    inv_n = 1.0 / RK_HEAD

    def stack(x):
        return jnp.concatenate([jnp.where(head0, x, 0.0), jnp.where(head0, 0.0, x)], axis=0)

    def fold(x):
        return x[:C] + x[C:]

    def each(fn, *lists):
        return [fn(*xs) for xs in zip(*lists)]

    cat0 = lambda *xs: jnp.concatenate(xs, axis=0)
    cat1 = lambda *xs: jnp.concatenate(xs, axis=1)
    lanes = [slice(pr * LANES, (pr + 1) * LANES) for pr in range(npair)]
    r = [r_ref[0, :, ln] for ln in lanes]
    k = [k_ref[0, :, ln] for ln in lanes]
    v = [v_ref[0, :, ln] for ln in lanes]
    a = [a_ref[0, :, ln] for ln in lanes]
    logw = [-jnp.exp(wl_ref[0, :, ln]) for ln in lanes]
    kk = [ki * kk_ref[:, ln] for ki, ln in zip(k, lanes)]
    k2 = [ki * (1.0 + (ai - 1.0) * ka_ref[:, ln]) for ki, ai, ln in zip(k, a, lanes)]
    sums = [_dot_exact_rhs(cat0(kki * kki, ri_ * k2i * rk_ref[:, ln]), bd_ones)
            for kki, ri_, k2i, ln in zip(kk, r, k2, lanes)]
    kk = each(lambda kki, si: kki / jnp.maximum(jnp.sqrt(si[:C]), 1e-12), kk, sums)
    bonus = each(lambda si, vi: si[C:] * vi, sums, v)
    b = each(lambda kki, ai: kki * ai, kk, a)
    if t_valid < C:
        logw = [jnp.where(valid, x, 0.0) for x in logw]
        b = [jnp.where(valid, x, 0.0) for x in b]
        k2 = [jnp.where(valid, x, 0.0) for x in k2]

    cum = [_dot_exact_lhs(tri_incl, x) for x in logw]
    gam = [jnp.exp(x) for x in cum]
    inv_gam = [jnp.exp(-x) for x in cum]
    gam_prev = each(lambda cu, lw: jnp.exp(cu - lw), cum, logw)
    g_last = [x[C - 1:C, :] for x in gam]

    bt = each(lambda x, ig: x * ig, b, inv_gam)
    kt = each(lambda x, ig: x * ig, k2, inv_gam)
    ats = each(lambda kki, gp: stack(-kki * gp), kk, gam_prev)
    bts, kts, vs = [stack(x) for x in bt], [stack(x) for x in kt], [stack(x) for x in v]
    rts = each(lambda ri_, gi: stack(ri_ * gi), r, gam)
    if C2 % LANES == 0:
        gram = each(lambda at_, rt_, bt_, kt_: dp(cat0(at_, rt_), cat0(bt_, kt_), 1, 1), ats, rts, bts, kts)
        g_ab, g_ak = [x[:C2, :C2] for x in gram], [x[:C2, C2:] for x in gram]
        g_rb, g_rk = [x[C2:, :C2] for x in gram], [x[C2:, C2:] for x in gram]
    else:
        g_ab, g_ak = each(lambda x, y_: dp(x, y_, 1, 1), ats, bts), each(lambda x, y_: dp(x, y_, 1, 1), ats, kts)
        g_rb, g_rk = each(lambda x, y_: dp(x, y_, 1, 1), rts, bts), each(lambda x, y_: dp(x, y_, 1, 1), rts, kts)
    a_ak = [jnp.where(strict, x, 0.0) for x in g_ak]
    m_rb = [jnp.where(incl, x, 0.0) for x in g_rb]
    m_rk = [jnp.where(incl, x, 0.0) for x in g_rk]

    x = [jnp.where(strict, g, 0.0) for g in g_ab]
    t_inv = [eye2 + xi for xi in x]
    for _ in range(int(math.log2(C)) - 1):
        x = [dp(xi, xi) for xi in x]
        t_inv = each(lambda ti, xi: ti + dp(ti, xi), t_inv, x)

    ws = each(dp, a_ak, vs)
    tp = each(lambda ti, at_, wi: dp(ti, cat1(at_, wi)), t_inv, ats, ws)
    mq = each(dp, m_rb, tp)
    mv = each(dp, m_rk, vs)
    q1 = each(lambda rt_, m_: fold(rt_ + m_[:, :LANES]), rts, mq)
    q2 = each(lambda m_, mv_: fold(m_[:, LANES:] + mv_), mq, mv)
    p1 = [fold(t[:, :LANES]) for t in tp]
    p2 = [fold(t[:, LANES:]) for t in tp]

    s = [s_scr[pr] for pr in range(npair)]
    y = each(lambda q1_, s_, q2_: ds(q1_, s_, 1, 1) + q2_, q1, s, q2)
    bp = each(lambda x_, gl: x_ * gl, bt, g_last)
    kp = each(lambda x_, gl: x_ * gl, kt, g_last)
    gm = each(lambda gl, p1_, bp_: jnp.where(eye, gl, 0.0) + jnp.where(same_head, dp(p1_, bp_, 0, 0), 0.0),
              g_last, p1, bp)
    hm = each(lambda p2_, v_, bp_, kp_: jnp.where(same_head, dp(cat0(p2_, v_), cat0(bp_, kp_), 0, 0), 0.0),
              p2, v, bp, kp)
    s_new = each(lambda s_, gm_, hm_: ds(s_, gm_) + hm_, s, gm, hm)
    for pr in range(npair):
        s_scr[pr] = s_new[pr]

    mu = [_dot_exact_rhs(yi, bd_ones) * inv_n for yi in y]
    d = each(lambda yi, mi: yi - mi, y, mu)
    var = [_dot_exact_rhs(di * di, bd_ones) * inv_n for di in d]
    for pr, ln in enumerate(lanes):
        yn = d[pr] * lax.rsqrt(var[pr] + GN_EPS) * lg_ref[:, ln] + lb_ref[:, ln]
        z_ref[0, :, ln] = (yn + bonus[pr]) * g_ref[0, :, ln]

    @pl.when(c == pl.num_programs(2) - 1)
    def _():
        sT_ref[0] = s_scr[...]


def _pair_state(s):
    bsz, h, n, _ = s.shape
    s = s.reshape(bsz, h // 2, 2, n, n)
    z = jnp.zeros_like(s[:, :, 0])
    top = jnp.concatenate([s[:, :, 0], z], axis=-1)
    bot = jnp.concatenate([z, s[:, :, 1]], axis=-1)
    return jnp.concatenate([top, bot], axis=-2)


def _unpair_state(sp):
    n = sp.shape[-1] // 2
    s = jnp.stack([sp[:, :, :n, :n], sp[:, :, n:, n:]], axis=2)
    return s.reshape(sp.shape[0], sp.shape[1] * 2, n, n)


def _rk_scan(r, k, v, wl, a, g, p, s0, chunk, t_valid, npair=8, passes=1, state_passes=3):
    bsz, t, d = r.shape
    width = npair * LANES
    ngroup = d // width
    nchunk = t // chunk
    tok = pl.BlockSpec((1, chunk, width), lambda b, h, c: (b, c, h))
    par = pl.BlockSpec((1, width), lambda b, h, c: (0, h))
    st = pl.BlockSpec((1, npair, LANES, LANES), lambda b, h, c: (b, h, 0, 0))
    vec = lambda x: x.reshape(1, d)
    z, s_t = pl.pallas_call(
        functools.partial(_rk_scan_kernel, chunk=chunk, t_valid=t_valid, npair=npair, passes=passes,
                          state_passes=state_passes),
        grid=(bsz, ngroup, nchunk),
        in_specs=[tok] * 6 + [par] * 5 + [st],
        out_specs=[tok, st],
        out_shape=[jax.ShapeDtypeStruct((bsz, t, d), F32),
                   jax.ShapeDtypeStruct((bsz, d // LANES, LANES, LANES), F32)],
        scratch_shapes=[pltpu.VMEM((npair, LANES, LANES), F32)],
        compiler_params=_cparams(("parallel", "parallel", "arbitrary")),
        name="rk_scan",
    )(r, k, v, wl, a, g, vec(p["k_k"]), vec(p["k_a"]), vec(p["r_k"]), vec(p["lnx_g"]), vec(p["lnx_b"]),
      _pair_state(s0))
    return z, _unpair_state(s_t)


def _mm_res_kernel(x_ref, w_ref, res_ref, o_ref):
    o_ref[...] = res_ref[...] + jnp.dot(x_ref[...].astype(BF16), w_ref[...], preferred_element_type=F32)


def _mm_res(x, w, res, tm):
    m, kd = x.shape
    n = w.shape[1]
    return pl.pallas_call(
        _mm_res_kernel,
        grid=(m // tm,),
        in_specs=[pl.BlockSpec((tm, kd), lambda i: (i, 0)), pl.BlockSpec((kd, n), lambda i: (0, 0)),
                  pl.BlockSpec((tm, n), lambda i: (i, 0))],
        out_specs=pl.BlockSpec((tm, n), lambda i: (i, 0)),
        out_shape=jax.ShapeDtypeStruct((m, n), F32),
        compiler_params=_cparams(("parallel",)),
        name="mm_res",
    )(x, w.astype(BF16), res)


def _ffn_kernel(x_ref, g_ref, wg_ref, wu_ref, wd_ref, o_ref):
    x = x_ref[...]
    xn = _rms(x, g_ref[...]).astype(BF16)
    gt = jnp.dot(xn, wg_ref[...], preferred_element_type=F32)
    up = jnp.dot(xn, wu_ref[...], preferred_element_type=F32)
    h = (jax.nn.silu(gt) * up).astype(BF16)
    o_ref[...] = x + jnp.dot(h, wd_ref[...], preferred_element_type=F32)


def _ffn(x, g, w_gu, w_down, tm):
    m, d = x.shape
    ff = w_down.shape[0]
    wg, wu = w_gu[:, :ff].astype(BF16), w_gu[:, ff:].astype(BF16)
    full = lambda a: pl.BlockSpec(a.shape, lambda i: (0,) * a.ndim)
    row = pl.BlockSpec((tm, d), lambda i: (i, 0))
    wd = w_down.astype(BF16)
    return pl.pallas_call(
        _ffn_kernel,
        grid=(m // tm,),
        in_specs=[row, pl.BlockSpec((1, d), lambda i: (0, 0)), full(wg), full(wu), full(wd)],
        out_specs=row,
        out_shape=jax.ShapeDtypeStruct((m, d), F32),
        compiler_params=_cparams(("parallel",)),
        name="ffn",
    )(x, g.reshape(1, d), wg, wu, wd)


SLOT_BLOCK = 128


def _moe_route_kernel(x_ref, g_ref, rt_ref, xn_ref, pos_ref, post_ref, gate_ref, cnt_ref):
    tm = x_ref.shape[0]
    xn = _rms(x_ref[...], g_ref[...])
    xn_ref[...] = xn.astype(BF16)
    logits = _dot(xn, rt_ref[...], passes=6)
    lt = logits.T[:N_EXPERTS]
    row = _iota(lt.shape, 0)
    v1 = jnp.max(lt, axis=0, keepdims=True)
    i1 = jnp.min(jnp.where(lt == v1, row, N_EXPERTS), axis=0, keepdims=True)
    rest = jnp.where(row == i1, NEG_INF, lt)
    v2 = jnp.max(rest, axis=0, keepdims=True)
    i2 = jnp.min(jnp.where(rest == v2, row, N_EXPERTS), axis=0, keepdims=True)
    e2 = jnp.exp(v2 - v1)
    w1 = 1.0 / (1.0 + e2)
    w2 = e2 / (1.0 + e2)
    sel1, sel2 = row == i1, row == i2
    mask = sel1 | sel2
    gate_ref[0] = jnp.where(sel1, w1, 0.0) + jnp.where(sel2, w2, 0.0)
    before = (_iota((tm, tm), 0) < _iota((tm, tm), 1)).astype(BF16)
    maskf = jnp.where(mask, 1.0, 0.0)
    rank = _dg(maskf.astype(BF16), before, 1, 0)
    pos = jnp.where(mask, rank, -1.0)
    pos_ref[0] = pos
    filler = jnp.full((LANES - N_EXPERTS, tm), -1.0, F32)
    post_ref[...] = jnp.concatenate([pos, filler], axis=0).T
    cnt = jnp.sum(maskf, axis=1, keepdims=True)
    cnt_ref[0] = jnp.broadcast_to(cnt, (N_EXPERTS, LANES)).astype(jnp.int32)


def _moe_expert_kernel(cnt_ref, x_ref, xn_ref, pos_ref, post_ref, gate_ref, wg_ref, wu_ref, wd_ref, o_ref):
    i = pl.program_id(0)
    e = pl.program_id(1)
    tm = x_ref.shape[0]

    @pl.when(e == 0)
    def _():
        o_ref[...] = x_ref[...]

    pos_row = pos_ref[0, pl.ds(e, 1), :]
    gate_row = gate_ref[0, pl.ds(e, 1), :]
    post = post_ref[...]
    pos_col = jnp.sum(jnp.where(_iota(post.shape, 1) == e, post, 0.0), axis=1, keepdims=True)

    def block(blk, carry):
        base = (blk * SLOT_BLOCK).astype(F32)
        pick = pos_row == base + _iota((SLOT_BLOCK, tm), 0).astype(F32)
        xe = jnp.dot(pick.astype(BF16), xn_ref[...], preferred_element_type=F32).astype(BF16)
        gt = jnp.dot(xe, wg_ref[0], preferred_element_type=F32)
        up = jnp.dot(xe, wu_ref[0], preferred_element_type=F32)
        h = (jax.nn.silu(gt) * up).astype(BF16)
        ye = jnp.dot(h, wd_ref[0], preferred_element_type=F32)
        gslot = jnp.sum(jnp.where(pick, gate_row, 0.0), axis=1, keepdims=True)
        put = pos_col == base + _iota((tm, SLOT_BLOCK), 1).astype(F32)
        o_ref[...] += jnp.dot(put.astype(BF16), (ye * gslot).astype(BF16), preferred_element_type=F32)
        return carry

    lax.fori_loop(0, (cnt_ref[i, e] + (SLOT_BLOCK - 1)) // SLOT_BLOCK, block, 0)


def _moe(x, g, router, w_gu, w_down, tm):
    m, d = x.shape
    ne, ffe = w_down.shape[0], w_down.shape[1]
    nt = m // tm
    wg, wu = w_gu[:, :, :ffe].astype(BF16), w_gu[:, :, ffe:].astype(BF16)
    wd = w_down.astype(BF16)
    rt = _pad_to(router, 1, LANES)
    f = jax.ShapeDtypeStruct
    row1 = pl.BlockSpec((tm, d), lambda i: (i, 0))
    xn, pos, post, gate, cnt = pl.pallas_call(
        _moe_route_kernel,
        grid=(nt,),
        in_specs=[row1, pl.BlockSpec((1, d), lambda i: (0, 0)), pl.BlockSpec(rt.shape, lambda i: (0, 0))],
        out_specs=[row1, pl.BlockSpec((1, ne, tm), lambda i: (i, 0, 0)),
                   pl.BlockSpec((tm, LANES), lambda i: (i, 0)), pl.BlockSpec((1, ne, tm), lambda i: (i, 0, 0)),
                   pl.BlockSpec((1, ne, LANES), lambda i: (i, 0, 0))],
        out_shape=[f((m, d), BF16), f((nt, ne, tm), F32), f((m, LANES), F32), f((nt, ne, tm), F32),
                   f((nt, ne, LANES), jnp.int32)],
        compiler_params=_cparams(("parallel",)),
        name="moe_route",
    )(x, g.reshape(1, d), rt)
    row = pl.BlockSpec((tm, d), lambda i, e, c: (i, 0))
    per_tile = pl.BlockSpec((1, ne, tm), lambda i, e, c: (i, 0, 0))
    grid_spec = pltpu.PrefetchScalarGridSpec(
        num_scalar_prefetch=1,
        grid=(nt, ne),
        in_specs=[row, row, per_tile, pl.BlockSpec((tm, LANES), lambda i, e, c: (i, 0)), per_tile,
                  pl.BlockSpec((1, d, ffe), lambda i, e, c: (e, 0, 0)),
                  pl.BlockSpec((1, d, ffe), lambda i, e, c: (e, 0, 0)),
                  pl.BlockSpec((1, ffe, d), lambda i, e, c: (e, 0, 0))],
        out_specs=row,
    )
    return pl.pallas_call(
        _moe_expert_kernel,
        grid_spec=grid_spec,
        out_shape=f((m, d), F32),
        compiler_params=_cparams(("parallel", "arbitrary")),
        name="moe",
    )(cnt[:, :, 0], x, xn, pos, post, gate, wg, wu, wd)


def _rope_tables(pos):
    inv = ROPE_THETA ** (-jnp.arange(0, ROT_DIM, 2, dtype=F32) / ROT_DIM)
    ang = pos.astype(F32)[:, None] * inv[None, :]
    cos, sin = jnp.cos(ang), jnp.sin(ang)
    ones = jnp.ones((pos.shape[0], AT_HEAD_DIM - ROT_DIM), F32)
    cos_g = jnp.concatenate([cos, cos, ones], axis=1)
    sin_g = jnp.concatenate([-sin, sin, 0.0 * ones], axis=1)
    return jnp.tile(cos_g, (1, 2)), jnp.tile(sin_g, (1, 2))


def _norm_rope(x, gain, cos, sin):
    lane = _iota(x.shape, 1)
    first = lane < AT_HEAD_DIM
    sq = x * x
    s0 = jnp.sum(jnp.where(first, sq, 0.0), axis=-1, keepdims=True)
    s1 = jnp.sum(jnp.where(first, 0.0, sq), axis=-1, keepdims=True)
    ms = jnp.where(first, s0, s1) * (1.0 / AT_HEAD_DIM)
    y = x * lax.rsqrt(ms + RMS_EPS) * gain
    half = ROT_DIM // 2
    in_lo = (lane % AT_HEAD_DIM) < half
    partner = jnp.where(in_lo, pltpu.roll(y, LANES - half, axis=1), pltpu.roll(y, half, axis=1))
    return y * cos + partner * sin


def _qkv_kernel(x_ref, g_ref, w_ref, qg_ref, kg_ref, cos_ref, sin_ref, *outs, transpose_k):
    j = pl.program_id(1)
    xn_scr = outs[-1]
    nh = x_ref.shape[1] // LANES

    @pl.when(j == 0)
    def _():
        xn_scr[...] = _rms(x_ref[...], g_ref[...]).astype(BF16)

    y = jnp.dot(xn_scr[...], w_ref[...], preferred_element_type=F32)
    heads = [slice(h * LANES, (h + 1) * LANES) for h in range(nh)]

    @pl.when(j == 0)
    def _():
        for hs in heads:
            outs[0][:, hs] = _norm_rope(y[:, hs], qg_ref[...], cos_ref[...], sin_ref[...])

    @pl.when(j == 1)
    def _():
        for hs in heads:
            kh = _norm_rope(y[:, hs], kg_ref[...], cos_ref[...], sin_ref[...])
            if transpose_k:
                outs[1][0, hs, :] = kh.T
                outs[2][:, hs] = kh.astype(BF16)
            else:
                outs[1][:, hs] = kh

    @pl.when(j == 2)
    def _():
        if transpose_k:
            outs[3][...] = y
            outs[4][...] = y.astype(BF16)
        else:
            outs[2][...] = y


def _qkv_proj(x, g, w_in, qn_g, kn_g, cos_t, sin_t, tm, seq, transpose_k):
    m, d = x.shape
    nt = cos_t.shape[0] // tm
    row = pl.BlockSpec((tm, d), lambda i, j: (i, 0))
    vec = pl.BlockSpec((1, LANES), lambda i, j: (0, 0))
    tab = pl.BlockSpec((tm, LANES), lambda i, j: (i % nt, 0))
    f = jax.ShapeDtypeStruct
    if transpose_k:
        bsz, ns = m // seq, seq // tm
        out_shape = [f((m, d), F32), f((bsz, d, seq), F32), f((m, d), BF16), f((m, d), F32), f((m, d), BF16)]
        out_specs = [row, pl.BlockSpec((1, d, tm), lambda i, j: (i // ns, 0, i % ns)), row, row, row]
    else:
        out_shape = [f((m, d), F32)] * 3
        out_specs = [row] * 3
    return pl.pallas_call(
        functools.partial(_qkv_kernel, transpose_k=transpose_k),
        grid=(m // tm, 3),
        in_specs=[row, pl.BlockSpec((1, d), lambda i, j: (0, 0)), pl.BlockSpec((d, d), lambda i, j: (0, j)),
                  vec, vec, tab, tab],
        out_specs=out_specs,
        out_shape=out_shape,
        scratch_shapes=[pltpu.VMEM((tm, d), BF16)],
        compiler_params=_cparams(("parallel", "arbitrary")),
        name="qkv_proj",
    )(x, g.reshape(1, d), w_in.astype(BF16), jnp.tile(qn_g, 2).reshape(1, LANES),
      jnp.tile(kn_g, 2).reshape(1, LANES), cos_t, sin_t)


def _diff_finish(o1, o2, lam, sub_g, lam_init):
    o = o1 - lam * o2
    o = o * lax.rsqrt(jnp.mean(o * o, axis=-1, keepdims=True) + RMS_EPS) * sub_g
    return o * (1.0 - lam_init)


def _branch_rows(q):
    first = _iota(q.shape, 1) < AT_HEAD_DIM
    return jnp.concatenate([jnp.where(first, q, 0.0), jnp.where(first, 0.0, q)], axis=0)


def _attn_prompt_kernel(lam_ref, q_ref, k_ref, v_ref, sg_ref, o_ref, vt_scr, m_ref, l_ref, acc_ref,
                        *, tq, tk, lam_init):
    qi = pl.program_id(2)

    @pl.when(qi == 0)
    def _():
        for j in range(vt_scr.shape[0]):
            vt_scr[j] = v_ref[0, j].astype(F32).T.astype(BF16)

    qs = _branch_rows(q_ref[0] * (ATTN_SCALE * LOG2E)).astype(BF16)
    m_ref[...] = jnp.full(m_ref.shape, NEG_INF, F32)
    l_ref[...] = jnp.zeros(l_ref.shape, F32)
    acc_ref[...] = jnp.zeros(acc_ref.shape, F32)

    def scores(j, masked):
        s = _dg(k_ref[0, j], qs, 1, 1)
        if masked:
            kpos = j * tk + _iota(s.shape, 0)
            qpos = qi * tq + (_iota(s.shape, 1) % tq)
            s = jnp.where(kpos <= qpos, s, NEG_INF)
        return s

    def update(j, s):
        m_prev = m_ref[...]
        m_new = jnp.maximum(m_prev, jnp.max(s, axis=0, keepdims=True))
        alpha = jnp.exp2(m_prev - m_new)
        p = jnp.exp2(s - m_new)
        l_ref[...] = alpha * l_ref[...] + jnp.sum(p, axis=0, keepdims=True)
        acc_ref[...] = alpha * acc_ref[...] + jnp.dot(vt_scr[j], p.astype(BF16),
                                                      preferred_element_type=F32)
        m_ref[...] = m_new

    j_last = (qi * tq + (tq - 1)) // tk

    def run(j0, n):
        ss = [scores(j0 + u, False) for u in range(n)]
        for u in range(n):
            update(j0 + u, ss[u])

    def body(jj, carry):
        run(4 * jj, 4)
        return carry

    lax.fori_loop(0, j_last // 4, body, 0)
    done = (j_last // 4) * 4

    @pl.when(j_last - done >= 2)
    def _():
        run(done, 2)

    @pl.when(j_last % 2 == 1)
    def _():
        run(j_last - 1, 1)

    update(j_last, scores(j_last, True))
    o = acc_ref[...] / l_ref[...]
    o = o[:, :tq] - lam_ref[0] * o[:, tq:]
    o = o * lax.rsqrt(jnp.mean(o * o, axis=0, keepdims=True) + RMS_EPS) * sg_ref[...]
    o_ref[0] = (o * (1.0 - lam_init)).T


def _attn_prompt(q, k_bf, v_bf, lam, sub_g, lam_init, bsz, seq, tq, tk):
    d = q.shape[1]
    nh = d // LANES
    nk = seq // tk
    kv = pl.BlockSpec((1, nk, tk, LANES), lambda b, h, i: (b, 0, 0, h))
    out = pl.pallas_call(
        functools.partial(_attn_prompt_kernel, tq=tq, tk=tk, lam_init=lam_init),
        grid=(bsz, nh, seq // tq),
        in_specs=[pl.BlockSpec(memory_space=pltpu.SMEM),
                  pl.BlockSpec((1, tq, LANES), lambda b, h, i: (b, i, h)), kv, kv,
                  pl.BlockSpec((LANES, 1), lambda b, h, i: (0, 0))],
        out_specs=pl.BlockSpec((1, tq, LANES), lambda b, h, i: (b, i, h)),
        out_shape=jax.ShapeDtypeStruct((bsz, seq, d), F32),
        scratch_shapes=[pltpu.VMEM((nk, LANES, tk), BF16), pltpu.VMEM((1, 2 * tq), F32),
                        pltpu.VMEM((1, 2 * tq), F32), pltpu.VMEM((LANES, 2 * tq), F32)],
        compiler_params=_cparams(("parallel", "parallel", "arbitrary")),
        name="attn_prompt",
    )(lam.reshape(1), q.reshape(bsz, seq, d), k_bf.reshape(bsz, nk, tk, d), v_bf.reshape(bsz, nk, tk, d),
      sub_g.reshape(LANES, 1))
    return out.reshape(bsz * seq, d)


def _attn_decode_kernel(pt_ref, lam_ref, q_ref, kn_ref, vn_ref, sg_ref, *rest, nq, nh, pp, lam_init):
    k_refs, v_refs = rest[:pp], rest[pp:2 * pp]
    o_ref, m_ref, l_ref, acc_ref = rest[2 * pp:]
    step = pl.program_id(1)
    rows = q_ref.shape[1] // nh
    rpb = rows // 2
    page = k_refs[0].shape[3]

    @pl.when(step == 0)
    def _():
        m_ref[...] = jnp.full(m_ref.shape, NEG_INF, F32)
        l_ref[...] = jnp.zeros(l_ref.shape, F32)
        acc_ref[...] = jnp.zeros(acc_ref.shape, F32)

    q = q_ref[0] * ATTN_SCALE
    keep = ((_iota(q.shape, 0) % rows) < rpb) == (_iota(q.shape, 1) < AT_HEAD_DIM)
    qs = jnp.where(keep, q, 0.0)
    hrows = [slice(h * rows, (h + 1) * rows) for h in range(nh)]
    bf = lambda x: x.astype(BF16)

    def update(s, pv_fn):
        m_prev = m_ref[...]
        m_new = jnp.maximum(m_prev, jnp.max(s, axis=-1, keepdims=True))
        alpha = jnp.exp(m_prev - m_new)
        p = jnp.exp(s - m_new)
        l_ref[...] = alpha * l_ref[...] + jnp.sum(p, axis=-1, keepdims=True)
        acc_ref[...] = alpha * acc_ref[...] + pv_fn(p)
        m_ref[...] = m_new

    s = jnp.concatenate(
        [jnp.concatenate([_dg(bf(qs[hr]), bf(k_refs[i][0, 0, h * LANES:(h + 1) * LANES, :]), 1, 0)
                          for h, hr in enumerate(hrows)], axis=0) for i in range(pp)], axis=1)

    def pv_pages(p):
        outs = []
        for h, hr in enumerate(hrows):
            o = None
            for i in range(pp):
                vh = bf(v_refs[i][0, 0, pl.ds(h, page, stride=nh), :])
                t = jnp.dot(bf(p[hr, i * page:(i + 1) * page]), vh, preferred_element_type=F32)
                o = t if o is None else o + t
            outs.append(o)
        return jnp.concatenate(outs, axis=0)

    update(s, pv_pages)

    @pl.when(step == pl.num_programs(1) - 1)
    def _():
        nk = kn_ref.shape[2]
        sn = jnp.concatenate([_dg(bf(qs[hr]), bf(kn_ref[0, h]), 1, 1) for h, hr in enumerate(hrows)],
                             axis=0)
        col = _iota(sn.shape, 1)
        causal = (col <= (_iota(sn.shape, 0) % rpb)) & (col < nq)
        update(jnp.where(causal, sn, NEG_INF),
               lambda p: jnp.concatenate([jnp.dot(bf(p[hr]), bf(vn_ref[0, h]), preferred_element_type=F32)
                                          for h, hr in enumerate(hrows)], axis=0))
        o = acc_ref[...] / l_ref[...]
        total = o.shape[0]
        o_ref[0] = _diff_finish(o, pltpu.roll(o, total - rpb, axis=0), lam_ref[0], sg_ref[...], lam_init)


def _attn_decode(q, k, v, cache_k, cache_v, layer, page_table, lam, sub_g, lam_init, bsz, nq, pp):
    d = q.shape[1]
    nh = d // LANES
    n_pages = page_table.shape[1]
    n_layers, n_pool, page = cache_k.shape[:3]
    kt = jnp.transpose(cache_k, (0, 1, 3, 4, 5, 2)).reshape(n_layers, n_pool, d, page)
    vv = cache_v.reshape(n_layers, n_pool, page * nh, LANES)
    rpb = -(-nq // 4) * 4
    heads_first = lambda x: jnp.transpose(x.reshape(bsz, nq, nh, LANES), (0, 2, 1, 3))
    qh = _pad_to(heads_first(q), 2, rpb)
    qh = jnp.concatenate([qh, qh], axis=2).reshape(bsz, nh * 2 * rpb, LANES)
    kh = _pad_to(heads_first(k), 2, 16)
    vh = _pad_to(heads_first(v), 2, 16)
    rows = nh * 2 * rpb
    qspec = pl.BlockSpec((1, rows, LANES), lambda b, s, pt: (b, 0, 0))
    nspec = pl.BlockSpec((1, nh, kh.shape[2], LANES), lambda b, s, pt: (b, 0, 0, 0))

    def page_spec(i, nrows):
        return pl.BlockSpec((1, 1, nrows, LANES), lambda b, s, pt: (layer, pt[b, s * pp + i], 0, 0))

    grid_spec = pltpu.PrefetchScalarGridSpec(
        num_scalar_prefetch=1,
        grid=(bsz, n_pages // pp),
        in_specs=([pl.BlockSpec(memory_space=pltpu.SMEM), qspec, nspec, nspec,
                   pl.BlockSpec((1, LANES), lambda b, s, pt: (0, 0))]
                  + [page_spec(i, d) for i in range(pp)] + [page_spec(i, page * nh) for i in range(pp)]),
        out_specs=qspec,
        scratch_shapes=[pltpu.VMEM((rows, 1), F32), pltpu.VMEM((rows, 1), F32),
                        pltpu.VMEM((rows, LANES), F32)],
    )
    out = pl.pallas_call(
        functools.partial(_attn_decode_kernel, nq=nq, nh=nh, pp=pp, lam_init=lam_init),
        grid_spec=grid_spec,
        out_shape=jax.ShapeDtypeStruct((bsz, rows, LANES), F32),
        compiler_params=_cparams(("parallel", "arbitrary")),
        name="attn_decode",
    )(page_table, lam.reshape(1), qh, kh, vh, sub_g.reshape(1, LANES), *([kt] * pp), *([vv] * pp))
    out = out.reshape(bsz, nh, 2, rpb, LANES)[:, :, 0, :nq]
    return jnp.transpose(out, (0, 2, 1, 3)).reshape(bsz * nq, d)


def _row_tile(m, want):
    t = min(m, want)
    while m % t:
        t //= 2
    return t


def _rwkv_layer(x, shift_prev, s0, norm_g, p):
    bsz, t, d = x.shape
    m = bsz * t
    if t % 256 == 0:
        (r, k, v, wl, a, g), shift_out = _rk_proj_fused(x, shift_prev, norm_g, p, 256)
    else:
        xn = _rmsnorm_rows(x.reshape(m, d), norm_g, _row_tile(m, 512)).reshape(bsz, t, d)
        prev = jnp.concatenate([shift_prev[:, None, :], xn[:, :-1]], axis=1)
        r, k, v, wl, a, g = _rk_proj(xn.reshape(m, d), prev.reshape(m, d), p, _row_tile(m, 256))
        shift_out = xn[:, -1]
    chunk = 64 if t % 64 == 0 else 8
    t_pad = -(-t // chunk) * chunk
    seqs = [u.reshape(bsz, t, d) for u in (r, k, v, wl, a, g)]
    if t_pad != t:
        seqs = [_pad_to(u, 1, chunk) for u in seqs]
    z, s_t = _rk_scan(*seqs, p, s0, chunk, t if t_pad != t else chunk)
    z = z[:, :t].reshape(m, d)
    out = _mm_res(z, p["wo"], x.reshape(m, d), _row_tile(m, 512))
    return out.reshape(bsz, t, d), shift_out, s_t


def _attn_layer(x, pos, norm_g, w_in, qn_g, kn_g, lam, lam_init, sub_g, wo, decode=None):
    bsz, t, d = x.shape
    m = bsz * t
    nh = d // LANES
    x2 = x.reshape(m, d)
    cos_t, sin_t = _rope_tables(pos)
    if decode is None:
        tm = _row_tile(t, 512)
        q, k_t, k_bf, v, v_bf = _qkv_proj(x2, norm_g, w_in, qn_g, kn_g, cos_t, sin_t, tm, t, True)
        o = _attn_prompt(q, k_bf, v_bf, lam, sub_g, lam_init, bsz, t, _row_tile(t, 256), _row_tile(t, 512))
        k_out = jnp.transpose(k_t.reshape(bsz, nh, 2, AT_HEAD_DIM, t), (0, 4, 1, 2, 3))
    else:
        cache_k, cache_v, layer, page_table = decode
        cos_t, sin_t = jnp.tile(cos_t, (bsz, 1)), jnp.tile(sin_t, (bsz, 1))
        q, k, v = _qkv_proj(x2, norm_g, w_in, qn_g, kn_g, cos_t, sin_t, m, t, False)
        pp = _row_tile(page_table.shape[1], 8)
        o = _attn_decode(q, k, v, cache_k, cache_v, layer, page_table, lam, sub_g, lam_init, bsz, t, pp)
        k_out = k.reshape(bsz, t, nh, 2, AT_HEAD_DIM)
    y = _mm_res(o, wo, x2, _row_tile(m, 512)).reshape(bsz, t, d)
    return y, k_out, v.reshape(bsz, t, nh, 2 * AT_HEAD_DIM)


def kernel(x_prompt, x_sample, state_wkv, state_shift, cache_k, cache_v, page_table,
           rk_norm_g, rk_mix, rk_wr, rk_wk, rk_wv, rk_w0, rk_w1, rk_w2, rk_a0, rk_a1, rk_a2,
           rk_g1, rk_g2, rk_k_k, rk_k_a, rk_r_k, rk_lnx_g, rk_lnx_b, rk_wo,
           ffn_norm_g, ffn_w_gu, ffn_w_down,
           at_norm_g, at_w_in, at_q_norm_g, at_k_norm_g, at_lq1, at_lk1, at_lq2, at_lk2,
           at_subln_g, at_wo,
           moe_norm_g, moe_router, moe_w_gu, moe_w_down):
    yp, ys = x_prompt, x_sample
    bp, tp, d = x_prompt.shape
    bs, ts, _ = x_sample.shape
    depth = rk_norm_g.shape[0] + at_norm_g.shape[0]
    past = page_table.shape[1] * cache_k.shape[2]
    pos_p = jnp.arange(tp)
    pos_s = past + jnp.arange(ts)
    nheads = d // RK_HEAD
    p_wkv, p_shift, p_k, p_v = [], [], [], []
    s_wkv, s_shift, s_k, s_v = [], [], [], []
    for i in range(depth):
        j = i // 2
        if i % 2 == 0:
            p = dict(mix=rk_mix[j], wr=rk_wr[j], wk=rk_wk[j], wv=rk_wv[j], w0=rk_w0[j], w1=rk_w1[j],
                     w2=rk_w2[j], a0=rk_a0[j], a1=rk_a1[j], a2=rk_a2[j], g1=rk_g1[j], g2=rk_g2[j],
                     k_k=rk_k_k[j], k_a=rk_k_a[j], r_k=rk_r_k[j].reshape(-1), lnx_g=rk_lnx_g[j],
                     lnx_b=rk_lnx_b[j], wo=rk_wo[j])
            yp, sh, st = _rwkv_layer(yp, jnp.zeros((bp, d), F32),
                                     jnp.zeros((bp, nheads, RK_HEAD, RK_HEAD), F32), rk_norm_g[j], p)
            p_shift.append(sh)
            p_wkv.append(st)
            ys, sh, st = _rwkv_layer(ys, state_shift[j], state_wkv[j], rk_norm_g[j], p)
            s_shift.append(sh)
            s_wkv.append(st)
            yp = _ffn(yp.reshape(bp * tp, d), ffn_norm_g[j], ffn_w_gu[j], ffn_w_down[j],
                      _row_tile(bp * tp, 512)).reshape(bp, tp, d)
            ys = _ffn(ys.reshape(bs * ts, d), ffn_norm_g[j], ffn_w_gu[j], ffn_w_down[j],
                      _row_tile(bs * ts, 512)).reshape(bs, ts, d)
        else:
            lam_init = 0.8 - 0.6 * math.exp(-0.3 * i)
            lam = (jnp.exp(jnp.sum(at_lq1[j] * at_lk1[j])) - jnp.exp(jnp.sum(at_lq2[j] * at_lk2[j]))
                   + lam_init)
            common = (at_norm_g[j], at_w_in[j], at_q_norm_g[j], at_k_norm_g[j], lam, lam_init,
                      at_subln_g[j], at_wo[j])
            yp, k_new, v_new = _attn_layer(yp, pos_p, *common)
            p_k.append(k_new)
            p_v.append(v_new)
            ys, k_new, v_new = _attn_layer(ys, pos_s, *common, decode=(cache_k, cache_v, j, page_table))
            s_k.append(k_new)
            s_v.append(v_new)
            yp = _moe(yp.reshape(bp * tp, d), moe_norm_g[j], moe_router[j], moe_w_gu[j], moe_w_down[j],
                      _row_tile(bp * tp, 1024)).reshape(bp, tp, d)
            ys = _moe(ys.reshape(bs * ts, d), moe_norm_g[j], moe_router[j], moe_w_gu[j], moe_w_down[j],
                      _row_tile(bs * ts, 1024)).reshape(bs, ts, d)
    return (yp, ys, jnp.stack(p_wkv), jnp.stack(p_shift), jnp.stack(p_k), jnp.stack(p_v),
            jnp.stack(s_wkv), jnp.stack(s_shift), jnp.stack(s_k), jnp.stack(s_v))
```

```python
import functools
import math

import jax
import jax.numpy as jnp
from jax import lax
from jax.experimental import pallas as pl
from jax.experimental.pallas import tpu as pltpu

F32 = jnp.float32
BF16 = jnp.bfloat16

RK_HEAD = 64
GN_EPS = 64e-5
AT_HEAD_DIM = 64
ROT_DIM = AT_HEAD_DIM // 4
ROPE_THETA = 500000.0
ATTN_SCALE = AT_HEAD_DIM ** -0.5
LOG2E = 1.4426950408889634
NEG_INF = -1e30
N_EXPERTS = 8
RMS_EPS = 1e-6
LANES = 128
VMEM_LIMIT = 56 * 1024 * 1024


def _cparams(sem):
    return pltpu.CompilerParams(dimension_semantics=sem, vmem_limit_bytes=VMEM_LIMIT)


def _dg(a, b, ca, cb):
    return lax.dot_general(a, b, (((ca,), (cb,)), ((), ())), preferred_element_type=F32)


def _split(x, n):
    parts = []
    for i in range(n):
        p = x.astype(BF16)
        parts.append(p)
        if i + 1 < n:
            x = x - p.astype(F32)
    return parts


def _dot(a, b, ca=1, cb=0, passes=3):
    if passes == 1:
        return _dg(a.astype(BF16), b.astype(BF16), ca, cb)
    if passes == 3:
        ah, al = _split(a, 2)
        bh, bl = _split(b, 2)
        return _dg(ah, bh, ca, cb) + (_dg(ah, bl, ca, cb) + _dg(al, bh, ca, cb))
    ah, am, al = _split(a, 3)
    bh, bm, bl = _split(b, 3)
    return (_dg(ah, bh, ca, cb) + (_dg(ah, bm, ca, cb) + _dg(am, bh, ca, cb))
            + (_dg(am, bm, ca, cb) + _dg(ah, bl, ca, cb) + _dg(al, bh, ca, cb)))


def _dot_exact_rhs(a, b_bf16, ca=1, cb=0):
    ah, al = _split(a, 2)
    return _dg(ah, b_bf16, ca, cb) + _dg(al, b_bf16, ca, cb)


def _dot_exact_lhs(a_bf16, b, ca=1, cb=0):
    bh, bl = _split(b, 2)
    return _dg(a_bf16, bh, ca, cb) + _dg(a_bf16, bl, ca, cb)


def _iota(shape, dim):
    return lax.broadcasted_iota(jnp.int32, shape, dim)


def _rms(x, g):
    return x * lax.rsqrt(jnp.mean(x * x, axis=-1, keepdims=True) + RMS_EPS) * g


def _rmsnorm_kernel(x_ref, g_ref, o_ref):
    o_ref[...] = _rms(x_ref[...], g_ref[...])


def _rmsnorm_rows(x, g, tm):
    m, d = x.shape
    return pl.pallas_call(
        _rmsnorm_kernel,
        grid=(m // tm,),
        in_specs=[pl.BlockSpec((tm, d), lambda i: (i, 0)), pl.BlockSpec((1, d), lambda i: (0, 0))],
        out_specs=pl.BlockSpec((tm, d), lambda i: (i, 0)),
        out_shape=jax.ShapeDtypeStruct((m, d), F32),
        compiler_params=_cparams(("parallel",)),
        name="rmsnorm",
    )(x, g.reshape(1, d))


def _rk_proj_kernel(xn_ref, prev_ref, *refs):
    _rk_proj_body(xn_ref[...], prev_ref[...], *refs)


def _rk_proj_fused_kernel(x_ref, halo_ref, shift_ref, ng_ref, *refs, tiles_per_seq):
    i = pl.program_id(0)
    ng = ng_ref[...]
    xn = _rms(x_ref[...], ng)
    halo = halo_ref[0]
    before = _rms(halo[halo.shape[0] - 1:, :], ng)
    first_prev = jnp.where(i % tiles_per_seq == 0, shift_ref[0], before)
    prev = jnp.where(_iota(xn.shape, 0) == 0, first_prev, pltpu.roll(xn, 1, axis=0))
    tail = refs[-1]
    tail[0] = xn[xn.shape[0] - tail.shape[1]:, :]
    _rk_proj_body(xn, prev, *refs[:-1])


def _rk_proj_body(xn, prev, mix_ref, wr_ref, wk_ref, wv_ref, w1_ref, w2_ref, w0_ref,
                  a1_ref, a2_ref, a0_ref, g1_ref, g2_ref,
                  r_ref, k_ref, v_ref, wl_ref, a_ref, g_ref):
    dx = prev - xn
    mix = mix_ref[...]

    def mixed(c):
        return (xn + dx * mix[c:c + 1, :]).astype(BF16)

    def mm(x, w_ref):
        return jnp.dot(x, w_ref[...], preferred_element_type=F32)

    r_ref[...] = mm(mixed(0), wr_ref)
    k_ref[...] = mm(mixed(2), wk_ref)
    v_ref[...] = mm(mixed(3), wv_ref)
    z = w0_ref[...] + mm(jnp.tanh(mm(mixed(1), w1_ref)).astype(BF16), w2_ref)
    wl_ref[...] = -jax.nn.softplus(-z) - 0.5
    a_ref[...] = jax.nn.sigmoid(a0_ref[...] + mm(mm(mixed(4), a1_ref).astype(BF16), a2_ref))
    g_ref[...] = mm(jax.nn.sigmoid(mm(mixed(5), g1_ref)).astype(BF16), g2_ref)


def _pad_to(x, axis, mult):
    pad = (-x.shape[axis]) % mult
    if pad == 0:
        return x
    widths = [(0, 0)] * x.ndim
    widths[axis] = (0, pad)
    return jnp.pad(x, widths)


def _rk_proj_weights(p, d):
    bf = lambda w: w.astype(BF16)
    w1, w2 = bf(_pad_to(p["w1"], 1, LANES)), bf(_pad_to(p["w2"], 0, LANES))
    a1, a2 = bf(_pad_to(p["a1"], 1, LANES)), bf(_pad_to(p["a2"], 0, LANES))
    g1, g2 = bf(_pad_to(p["g1"], 1, LANES)), bf(_pad_to(p["g2"], 0, LANES))
    return (p["mix"], bf(p["wr"]), bf(p["wk"]), bf(p["wv"]), w1, w2, p["w0"].reshape(1, d),
            a1, a2, p["a0"].reshape(1, d), g1, g2)


def _rk_proj(xn, prev, p, tm):
    m, d = xn.shape
    row = pl.BlockSpec((tm, d), lambda i: (i, 0))
    full = lambda a: pl.BlockSpec(a.shape, lambda i: (0,) * a.ndim)
    weights = _rk_proj_weights(p, d)
    return pl.pallas_call(
        _rk_proj_kernel,
        grid=(m // tm,),
        in_specs=[row, row] + [full(a) for a in weights],
        out_specs=[row] * 6,
        out_shape=[jax.ShapeDtypeStruct((m, d), F32)] * 6,
        compiler_params=_cparams(("parallel",)),
        name="rk_proj",
    )(xn, prev, *weights)


SUBLANES = 8


def _rk_proj_fused(x, shift_prev, norm_g, p, tm):
    bsz, t, d = x.shape
    m = bsz * t
    nt = t // tm
    row = pl.BlockSpec((tm, d), lambda i: (i, 0))
    full = lambda a: pl.BlockSpec(a.shape, lambda i: (0,) * a.ndim)
    weights = _rk_proj_weights(p, d)
    halo = pl.BlockSpec((1, SUBLANES, d), lambda i: (jnp.maximum(i * (tm // SUBLANES) - 1, 0), 0, 0))
    outs = pl.pallas_call(
        functools.partial(_rk_proj_fused_kernel, tiles_per_seq=nt),
        grid=(m // tm,),
        in_specs=[row, halo, pl.BlockSpec((1, 1, d), lambda i: (i // nt, 0, 0)),
                  pl.BlockSpec((1, d), lambda i: (0, 0))] + [full(a) for a in weights],
        out_specs=[row] * 6 + [pl.BlockSpec((1, SUBLANES, d), lambda i: (i // nt, 0, 0))],
        out_shape=[jax.ShapeDtypeStruct((m, d), F32)] * 6 + [jax.ShapeDtypeStruct((bsz, SUBLANES, d), F32)],
        compiler_params=_cparams(("arbitrary",)),
        name="rk_proj_fused",
    )(x.reshape(m, d), x.reshape(m // SUBLANES, SUBLANES, d), shift_prev.reshape(bsz, 1, d),
      norm_g.reshape(1, d), *weights)
    return outs[:6], outs[6][:, SUBLANES - 1]


def _rk_scan_kernel(r_ref, k_ref, v_ref, wl_ref, a_ref, g_ref, kk_ref, ka_ref, rk_ref, lg_ref, lb_ref,
                    s0_ref, z_ref, sT_ref, s_scr, *, chunk, t_valid, npair, passes, state_passes):
    c = pl.program_id(2)
    C = chunk
    C2 = 2 * C

    @pl.when(c == 0)
    def _():
        s_scr[...] = s0_ref[0]

    lane = _iota((C, LANES), 1)
    head0 = lane < RK_HEAD
    same_head = ((_iota((LANES, LANES), 0) < RK_HEAD) == (_iota((LANES, LANES), 1) < RK_HEAD))
    bd_ones = same_head.astype(BF16)
    eye = _iota((LANES, LANES), 0) == _iota((LANES, LANES), 1)
    tri_incl = (_iota((C, C), 0) >= _iota((C, C), 1)).astype(BF16)
    ri = _iota((C2, C2), 0)
    ci = _iota((C2, C2), 1)
    same_blk = (ri < C) == (ci < C)
    strict = same_blk & (ri > ci)
    incl = same_blk & (ri >= ci)
    eye2 = jnp.where(ri == ci, 1.0, 0.0)
    valid = _iota((C, LANES), 0) < t_valid
    dp = functools.partial(_dot, passes=passes)
    ds = functools.partial(_dot, passes=state_passes)
    inv_n = 1.0 / RK_HEAD

    def stack(x):
        return jnp.concatenate([jnp.where(head0, x, 0.0), jnp.where(head0, 0.0, x)], axis=0)

    def fold(x):
        return x[:C] + x[C:]

    def each(fn, *lists):
        return [fn(*xs) for xs in zip(*lists)]

    cat0 = lambda *xs: jnp.concatenate(xs, axis=0)
    cat1 = lambda *xs: jnp.concatenate(xs, axis=1)
    lanes = [slice(pr * LANES, (pr + 1) * LANES) for pr in range(npair)]
    r = [r_ref[0, :, ln] for ln in lanes]
    k = [k_ref[0, :, ln] for ln in lanes]
    v = [v_ref[0, :, ln] for ln in lanes]
    a = [a_ref[0, :, ln] for ln in lanes]
    logw = [-jnp.exp(wl_ref[0, :, ln]) for ln in lanes]
    kk = [ki * kk_ref[:, ln] for ki, ln in zip(k, lanes)]
    k2 = [ki * (1.0 + (ai - 1.0) * ka_ref[:, ln]) for ki, ai, ln in zip(k, a, lanes)]
    sums = [_dot_exact_rhs(cat0(kki * kki, ri_ * k2i * rk_ref[:, ln]), bd_ones)
            for kki, ri_, k2i, ln in zip(kk, r, k2, lanes)]
    kk = each(lambda kki, si: kki / jnp.maximum(jnp.sqrt(si[:C]), 1e-12), kk, sums)
    bonus = each(lambda si, vi: si[C:] * vi, sums, v)
    b = each(lambda kki, ai: kki * ai, kk, a)
    if t_valid < C:
        logw = [jnp.where(valid, x, 0.0) for x in logw]
        b = [jnp.where(valid, x, 0.0) for x in b]
        k2 = [jnp.where(valid, x, 0.0) for x in k2]

    cum = [_dot_exact_lhs(tri_incl, x) for x in logw]
    gam = [jnp.exp(x) for x in cum]
    inv_gam = [jnp.exp(-x) for x in cum]
    gam_prev = each(lambda cu, lw: jnp.exp(cu - lw), cum, logw)
    g_last = [x[C - 1:C, :] for x in gam]

    bt = each(lambda x, ig: x * ig, b, inv_gam)
    kt = each(lambda x, ig: x * ig, k2, inv_gam)
    ats = each(lambda kki, gp: stack(-kki * gp), kk, gam_prev)
    bts, kts, vs = [stack(x) for x in bt], [stack(x) for x in kt], [stack(x) for x in v]
    rts = each(lambda ri_, gi: stack(ri_ * gi), r, gam)
    if C2 % LANES == 0:
        gram = each(lambda at_, rt_, bt_, kt_: dp(cat0(at_, rt_), cat0(bt_, kt_), 1, 1), ats, rts, bts, kts)
        g_ab, g_ak = [x[:C2, :C2] for x in gram], [x[:C2, C2:] for x in gram]
        g_rb, g_rk = [x[C2:, :C2] for x in gram], [x[C2:, C2:] for x in gram]
    else:
        g_ab, g_ak = each(lambda x, y_: dp(x, y_, 1, 1), ats, bts), each(lambda x, y_: dp(x, y_, 1, 1), ats, kts)
        g_rb, g_rk = each(lambda x, y_: dp(x, y_, 1, 1), rts, bts), each(lambda x, y_: dp(x, y_, 1, 1), rts, kts)
    a_ak = [jnp.where(strict, x, 0.0) for x in g_ak]
    m_rb = [jnp.where(incl, x, 0.0) for x in g_rb]
    m_rk = [jnp.where(incl, x, 0.0) for x in g_rk]

    x = [jnp.where(strict, g, 0.0) for g in g_ab]
    t_inv = [eye2 + xi for xi in x]
    for _ in range(int(math.log2(C)) - 1):
        x = [dp(xi, xi) for xi in x]
        t_inv = each(lambda ti, xi: ti + dp(ti, xi), t_inv, x)

    ws = each(dp, a_ak, vs)
    tp = each(lambda ti, at_, wi: dp(ti, cat1(at_, wi)), t_inv, ats, ws)
    mq = each(dp, m_rb, tp)
    mv = each(dp, m_rk, vs)
    q1 = each(lambda rt_, m_: fold(rt_ + m_[:, :LANES]), rts, mq)
    q2 = each(lambda m_, mv_: fold(m_[:, LANES:] + mv_), mq, mv)
    p1 = [fold(t[:, :LANES]) for t in tp]
    p2 = [fold(t[:, LANES:]) for t in tp]

    s = [s_scr[pr] for pr in range(npair)]
    y = each(lambda q1_, s_, q2_: ds(q1_, s_, 1, 1) + q2_, q1, s, q2)
    bp = each(lambda x_, gl: x_ * gl, bt, g_last)
    kp = each(lambda x_, gl: x_ * gl, kt, g_last)
    gm = each(lambda gl, p1_, bp_: jnp.where(eye, gl, 0.0) + jnp.where(same_head, dp(p1_, bp_, 0, 0), 0.0),
              g_last, p1, bp)
    hm = each(lambda p2_, v_, bp_, kp_: jnp.where(same_head, dp(cat0(p2_, v_), cat0(bp_, kp_), 0, 0), 0.0),
              p2, v, bp, kp)
    s_new = each(lambda s_, gm_, hm_: ds(s_, gm_) + hm_, s, gm, hm)
    for pr in range(npair):
        s_scr[pr] = s_new[pr]

    mu = [_dot_exact_rhs(yi, bd_ones) * inv_n for yi in y]
    d = each(lambda yi, mi: yi - mi, y, mu)
    var = [_dot_exact_rhs(di * di, bd_ones) * inv_n for di in d]
    for pr, ln in enumerate(lanes):
        yn = d[pr] * lax.rsqrt(var[pr] + GN_EPS) * lg_ref[:, ln] + lb_ref[:, ln]
        z_ref[0, :, ln] = (yn + bonus[pr]) * g_ref[0, :, ln]

    @pl.when(c == pl.num_programs(2) - 1)
    def _():
        sT_ref[0] = s_scr[...]


def _pair_state(s):
    bsz, h, n, _ = s.shape
    s = s.reshape(bsz, h // 2, 2, n, n)
    z = jnp.zeros_like(s[:, :, 0])
    top = jnp.concatenate([s[:, :, 0], z], axis=-1)
    bot = jnp.concatenate([z, s[:, :, 1]], axis=-1)
    return jnp.concatenate([top, bot], axis=-2)


def _unpair_state(sp):
    n = sp.shape[-1] // 2
    s = jnp.stack([sp[:, :, :n, :n], sp[:, :, n:, n:]], axis=2)
    return s.reshape(sp.shape[0], sp.shape[1] * 2, n, n)


def _rk_scan(r, k, v, wl, a, g, p, s0, chunk, t_valid, npair=8, passes=1, state_passes=3):
    bsz, t, d = r.shape
    width = npair * LANES
    ngroup = d // width
    nchunk = t // chunk
    tok = pl.BlockSpec((1, chunk, width), lambda b, h, c: (b, c, h))
    par = pl.BlockSpec((1, width), lambda b, h, c: (0, h))
    st = pl.BlockSpec((1, npair, LANES, LANES), lambda b, h, c: (b, h, 0, 0))
    vec = lambda x: x.reshape(1, d)
    z, s_t = pl.pallas_call(
        functools.partial(_rk_scan_kernel, chunk=chunk, t_valid=t_valid, npair=npair, passes=passes,
                          state_passes=state_passes),
        grid=(bsz, ngroup, nchunk),
        in_specs=[tok] * 6 + [par] * 5 + [st],
        out_specs=[tok, st],
        out_shape=[jax.ShapeDtypeStruct((bsz, t, d), F32),
                   jax.ShapeDtypeStruct((bsz, d // LANES, LANES, LANES), F32)],
        scratch_shapes=[pltpu.VMEM((npair, LANES, LANES), F32)],
        compiler_params=_cparams(("parallel", "parallel", "arbitrary")),
        name="rk_scan",
    )(r, k, v, wl, a, g, vec(p["k_k"]), vec(p["k_a"]), vec(p["r_k"]), vec(p["lnx_g"]), vec(p["lnx_b"]),
      _pair_state(s0))
    return z, _unpair_state(s_t)


def _mm_res_kernel(x_ref, w_ref, res_ref, o_ref):
    o_ref[...] = res_ref[...] + jnp.dot(x_ref[...].astype(BF16), w_ref[...], preferred_element_type=F32)


def _mm_res(x, w, res, tm):
    m, kd = x.shape
    n = w.shape[1]
    return pl.pallas_call(
        _mm_res_kernel,
        grid=(m // tm,),
        in_specs=[pl.BlockSpec((tm, kd), lambda i: (i, 0)), pl.BlockSpec((kd, n), lambda i: (0, 0)),
                  pl.BlockSpec((tm, n), lambda i: (i, 0))],
        out_specs=pl.BlockSpec((tm, n), lambda i: (i, 0)),
        out_shape=jax.ShapeDtypeStruct((m, n), F32),
        compiler_params=_cparams(("parallel",)),
        name="mm_res",
    )(x, w.astype(BF16), res)


def _ffn_kernel(x_ref, g_ref, wg_ref, wu_ref, wd_ref, o_ref):
    x = x_ref[...]
    xn = _rms(x, g_ref[...]).astype(BF16)
    gt = jnp.dot(xn, wg_ref[...], preferred_element_type=F32)
    up = jnp.dot(xn, wu_ref[...], preferred_element_type=F32)
    h = (jax.nn.silu(gt) * up).astype(BF16)
    o_ref[...] = x + jnp.dot(h, wd_ref[...], preferred_element_type=F32)


def _ffn(x, g, w_gu, w_down, tm):
    m, d = x.shape
    ff = w_down.shape[0]
    wg, wu = w_gu[:, :ff].astype(BF16), w_gu[:, ff:].astype(BF16)
    full = lambda a: pl.BlockSpec(a.shape, lambda i: (0,) * a.ndim)
    row = pl.BlockSpec((tm, d), lambda i: (i, 0))
    wd = w_down.astype(BF16)
    return pl.pallas_call(
        _ffn_kernel,
        grid=(m // tm,),
        in_specs=[row, pl.BlockSpec((1, d), lambda i: (0, 0)), full(wg), full(wu), full(wd)],
        out_specs=row,
        out_shape=jax.ShapeDtypeStruct((m, d), F32),
        compiler_params=_cparams(("parallel",)),
        name="ffn",
    )(x, g.reshape(1, d), wg, wu, wd)


SLOT_BLOCK = 128


def _moe_route_kernel(x_ref, g_ref, rt_ref, xn_ref, pos_ref, post_ref, gate_ref, cnt_ref):
    tm = x_ref.shape[0]
    xn = _rms(x_ref[...], g_ref[...])
    xn_ref[...] = xn.astype(BF16)
    logits = _dot(xn, rt_ref[...], passes=6)
    lt = logits.T[:N_EXPERTS]
    row = _iota(lt.shape, 0)
    v1 = jnp.max(lt, axis=0, keepdims=True)
    i1 = jnp.min(jnp.where(lt == v1, row, N_EXPERTS), axis=0, keepdims=True)
    rest = jnp.where(row == i1, NEG_INF, lt)
    v2 = jnp.max(rest, axis=0, keepdims=True)
    i2 = jnp.min(jnp.where(rest == v2, row, N_EXPERTS), axis=0, keepdims=True)
    e2 = jnp.exp(v2 - v1)
    w1 = 1.0 / (1.0 + e2)
    w2 = e2 / (1.0 + e2)
    sel1, sel2 = row == i1, row == i2
    mask = sel1 | sel2
    gate_ref[0] = jnp.where(sel1, w1, 0.0) + jnp.where(sel2, w2, 0.0)
    before = (_iota((tm, tm), 0) < _iota((tm, tm), 1)).astype(BF16)
    maskf = jnp.where(mask, 1.0, 0.0)
    rank = _dg(maskf.astype(BF16), before, 1, 0)
    pos = jnp.where(mask, rank, -1.0)
    pos_ref[0] = pos
    filler = jnp.full((LANES - N_EXPERTS, tm), -1.0, F32)
    post_ref[...] = jnp.concatenate([pos, filler], axis=0).T
    cnt = jnp.sum(maskf, axis=1, keepdims=True)
    cnt_ref[0] = jnp.broadcast_to(cnt, (N_EXPERTS, LANES)).astype(jnp.int32)


def _moe_expert_kernel(cnt_ref, x_ref, xn_ref, pos_ref, post_ref, gate_ref, wg_ref, wu_ref, wd_ref, o_ref):
    i = pl.program_id(0)
    e = pl.program_id(1)
    tm = x_ref.shape[0]

    @pl.when(e == 0)
    def _():
        o_ref[...] = x_ref[...]

    pos_row = pos_ref[0, pl.ds(e, 1), :]
    gate_row = gate_ref[0, pl.ds(e, 1), :]
    post = post_ref[...]
    pos_col = jnp.sum(jnp.where(_iota(post.shape, 1) == e, post, 0.0), axis=1, keepdims=True)

    def block(blk, carry):
        base = (blk * SLOT_BLOCK).astype(F32)
        pick = pos_row == base + _iota((SLOT_BLOCK, tm), 0).astype(F32)
        xe = jnp.dot(pick.astype(BF16), xn_ref[...], preferred_element_type=F32).astype(BF16)
        gt = jnp.dot(xe, wg_ref[0], preferred_element_type=F32)
        up = jnp.dot(xe, wu_ref[0], preferred_element_type=F32)
        h = (jax.nn.silu(gt) * up).astype(BF16)
        ye = jnp.dot(h, wd_ref[0], preferred_element_type=F32)
        gslot = jnp.sum(jnp.where(pick, gate_row, 0.0), axis=1, keepdims=True)
        put = pos_col == base + _iota((tm, SLOT_BLOCK), 1).astype(F32)
        o_ref[...] += jnp.dot(put.astype(BF16), (ye * gslot).astype(BF16), preferred_element_type=F32)
        return carry

    lax.fori_loop(0, (cnt_ref[i, e] + (SLOT_BLOCK - 1)) // SLOT_BLOCK, block, 0)


def _moe(x, g, router, w_gu, w_down, tm):
    m, d = x.shape
    ne, ffe = w_down.shape[0], w_down.shape[1]
    nt = m // tm
    wg, wu = w_gu[:, :, :ffe].astype(BF16), w_gu[:, :, ffe:].astype(BF16)
    wd = w_down.astype(BF16)
    rt = _pad_to(router, 1, LANES)
    f = jax.ShapeDtypeStruct
    row1 = pl.BlockSpec((tm, d), lambda i: (i, 0))
    xn, pos, post, gate, cnt = pl.pallas_call(
        _moe_route_kernel,
        grid=(nt,),
        in_specs=[row1, pl.BlockSpec((1, d), lambda i: (0, 0)), pl.BlockSpec(rt.shape, lambda i: (0, 0))],
        out_specs=[row1, pl.BlockSpec((1, ne, tm), lambda i: (i, 0, 0)),
                   pl.BlockSpec((tm, LANES), lambda i: (i, 0)), pl.BlockSpec((1, ne, tm), lambda i: (i, 0, 0)),
                   pl.BlockSpec((1, ne, LANES), lambda i: (i, 0, 0))],
        out_shape=[f((m, d), BF16), f((nt, ne, tm), F32), f((m, LANES), F32), f((nt, ne, tm), F32),
                   f((nt, ne, LANES), jnp.int32)],
        compiler_params=_cparams(("parallel",)),
        name="moe_route",
    )(x, g.reshape(1, d), rt)
    row = pl.BlockSpec((tm, d), lambda i, e, c: (i, 0))
    per_tile = pl.BlockSpec((1, ne, tm), lambda i, e, c: (i, 0, 0))
    grid_spec = pltpu.PrefetchScalarGridSpec(
        num_scalar_prefetch=1,
        grid=(nt, ne),
        in_specs=[row, row, per_tile, pl.BlockSpec((tm, LANES), lambda i, e, c: (i, 0)), per_tile,
                  pl.BlockSpec((1, d, ffe), lambda i, e, c: (e, 0, 0)),
                  pl.BlockSpec((1, d, ffe), lambda i, e, c: (e, 0, 0)),
                  pl.BlockSpec((1, ffe, d), lambda i, e, c: (e, 0, 0))],
        out_specs=row,
    )
    return pl.pallas_call(
        _moe_expert_kernel,
        grid_spec=grid_spec,
        out_shape=f((m, d), F32),
        compiler_params=_cparams(("parallel", "arbitrary")),
        name="moe",
    )(cnt[:, :, 0], x, xn, pos, post, gate, wg, wu, wd)


def _rope_tables(pos):
    inv = ROPE_THETA ** (-jnp.arange(0, ROT_DIM, 2, dtype=F32) / ROT_DIM)
    ang = pos.astype(F32)[:, None] * inv[None, :]
    cos, sin = jnp.cos(ang), jnp.sin(ang)
    ones = jnp.ones((pos.shape[0], AT_HEAD_DIM - ROT_DIM), F32)
    cos_g = jnp.concatenate([cos, cos, ones], axis=1)
    sin_g = jnp.concatenate([-sin, sin, 0.0 * ones], axis=1)
    return jnp.tile(cos_g, (1, 2)), jnp.tile(sin_g, (1, 2))


def _norm_rope(x, gain, cos, sin):
    lane = _iota(x.shape, 1)
    first = lane < AT_HEAD_DIM
    sq = x * x
    s0 = jnp.sum(jnp.where(first, sq, 0.0), axis=-1, keepdims=True)
    s1 = jnp.sum(jnp.where(first, 0.0, sq), axis=-1, keepdims=True)
    ms = jnp.where(first, s0, s1) * (1.0 / AT_HEAD_DIM)
    y = x * lax.rsqrt(ms + RMS_EPS) * gain
    half = ROT_DIM // 2
    in_lo = (lane % AT_HEAD_DIM) < half
    partner = jnp.where(in_lo, pltpu.roll(y, LANES - half, axis=1), pltpu.roll(y, half, axis=1))
    return y * cos + partner * sin


def _qkv_kernel(x_ref, g_ref, w_ref, qg_ref, kg_ref, cos_ref, sin_ref, *outs, transpose_k):
    j = pl.program_id(1)
    xn_scr = outs[-1]
    nh = x_ref.shape[1] // LANES

    @pl.when(j == 0)
    def _():
        xn_scr[...] = _rms(x_ref[...], g_ref[...]).astype(BF16)

    y = jnp.dot(xn_scr[...], w_ref[...], preferred_element_type=F32)
    heads = [slice(h * LANES, (h + 1) * LANES) for h in range(nh)]

    @pl.when(j == 0)
    def _():
        for hs in heads:
            outs[0][:, hs] = _norm_rope(y[:, hs], qg_ref[...], cos_ref[...], sin_ref[...])

    @pl.when(j == 1)
    def _():
        for hs in heads:
            kh = _norm_rope(y[:, hs], kg_ref[...], cos_ref[...], sin_ref[...])
            if transpose_k:
                kt = kh.T
                outs[1][0, hs, :] = kt
                outs[2][0, 0, hs, :] = kt.astype(BF16)
            else:
                outs[1][:, hs] = kh

    @pl.when(j == 2)
    def _():
        if transpose_k:
            outs[3][...] = y
            outs[4][...] = y.astype(BF16)
        else:
            outs[2][...] = y


def _qkv_proj(x, g, w_in, qn_g, kn_g, cos_t, sin_t, tm, seq, transpose_k):
    m, d = x.shape
    nt = cos_t.shape[0] // tm
    row = pl.BlockSpec((tm, d), lambda i, j: (i, 0))
    vec = pl.BlockSpec((1, LANES), lambda i, j: (0, 0))
    tab = pl.BlockSpec((tm, LANES), lambda i, j: (i % nt, 0))
    f = jax.ShapeDtypeStruct
    if transpose_k:
        bsz, ns = m // seq, seq // tm
        out_shape = [f((m, d), F32), f((bsz, d, seq), F32), f((bsz, ns, d, tm), BF16), f((m, d), F32),
                     f((m, d), BF16)]
        out_specs = [row, pl.BlockSpec((1, d, tm), lambda i, j: (i // ns, 0, i % ns)),
                     pl.BlockSpec((1, 1, d, tm), lambda i, j: (i // ns, i % ns, 0, 0)), row, row]
    else:
        out_shape = [f((m, d), F32)] * 3
        out_specs = [row] * 3
    return pl.pallas_call(
        functools.partial(_qkv_kernel, transpose_k=transpose_k),
        grid=(m // tm, 3),
        in_specs=[row, pl.BlockSpec((1, d), lambda i, j: (0, 0)), pl.BlockSpec((d, d), lambda i, j: (0, j)),
                  vec, vec, tab, tab],
        out_specs=out_specs,
        out_shape=out_shape,
        scratch_shapes=[pltpu.VMEM((tm, d), BF16)],
        compiler_params=_cparams(("parallel", "arbitrary")),
        name="qkv_proj",
    )(x, g.reshape(1, d), w_in.astype(BF16), jnp.tile(qn_g, 2).reshape(1, LANES),
      jnp.tile(kn_g, 2).reshape(1, LANES), cos_t, sin_t)


def _diff_finish(o1, o2, lam, sub_g, lam_init):
    o = o1 - lam * o2
    o = o * lax.rsqrt(jnp.mean(o * o, axis=-1, keepdims=True) + RMS_EPS) * sub_g
    return o * (1.0 - lam_init)


def _branch_rows(q):
    first = _iota(q.shape, 1) < AT_HEAD_DIM
    return jnp.concatenate([jnp.where(first, q, 0.0), jnp.where(first, 0.0, q)], axis=0)


def _attn_prompt_kernel(lam_ref, q_ref, kt_ref, v_ref, sg_ref, o_ref, m_ref, l_ref, acc_ref,
                        *, tq, tk, lam_init):
    qi = pl.program_id(2)
    qs = _branch_rows(q_ref[0] * (ATTN_SCALE * LOG2E)).astype(BF16)
    m_ref[...] = jnp.full(m_ref.shape, NEG_INF, F32)
    l_ref[...] = jnp.zeros(l_ref.shape, F32)
    acc_ref[...] = jnp.zeros(acc_ref.shape, F32)

    def scores(j, diag=None):
        if diag is None:
            return _dg(qs, kt_ref[0, j], 1, 0)
        s = _dg(qs, kt_ref[0, j, :, diag * tq:(diag + 1) * tq], 1, 0)
        return jnp.where(_iota(s.shape, 1) <= (_iota(s.shape, 0) % tq), s, NEG_INF)

    def update(s, v):
        m_prev = m_ref[...]
        m_new = jnp.maximum(m_prev, jnp.max(s, axis=-1, keepdims=True))
        alpha = jnp.exp2(m_prev - m_new)
        p = jnp.exp2(s - jnp.concatenate([m_new] * (s.shape[1] // LANES), axis=1))
        l_ref[...] = alpha * l_ref[...] + jnp.sum(p, axis=-1, keepdims=True)
        acc_ref[...] = alpha * acc_ref[...] + jnp.dot(p.astype(BF16), v, preferred_element_type=F32)
        m_ref[...] = m_new

    j_last = (qi * tq + (tq - 1)) // tk

    def run(j0, n):
        ss = [scores(j0 + u) for u in range(n)]
        for u in range(n):
            update(ss[u], v_ref[0, j0 + u])

    def body(jj, carry):
        run(4 * jj, 4)
        return carry

    lax.fori_loop(0, j_last // 4, body, 0)
    done = (j_last // 4) * 4

    @pl.when(j_last - done >= 2)
    def _():
        run(done, 2)

    @pl.when(j_last % 2 == 1)
    def _():
        run(j_last - 1, 1)

    nsub = tk // tq
    for c in range(nsub):
        @pl.when(qi % nsub == c)
        def _(c=c):
            if c:
                update(_dg(qs, kt_ref[0, j_last, :, :c * tq], 1, 0), v_ref[0, j_last, :c * tq, :])
            update(scores(j_last, diag=c), v_ref[0, j_last, c * tq:(c + 1) * tq, :])

    o = acc_ref[...] / l_ref[...]
    o_ref[0] = _diff_finish(o[:tq], o[tq:], lam_ref[0], sg_ref[...], lam_init)


def _attn_prompt(q, kt_bf, v_bf, lam, sub_g, lam_init, bsz, seq, tq):
    d = q.shape[1]
    nh = d // LANES
    nk, tk = kt_bf.shape[1], kt_bf.shape[3]
    out = pl.pallas_call(
        functools.partial(_attn_prompt_kernel, tq=tq, tk=tk, lam_init=lam_init),
        grid=(bsz, nh, seq // tq),
        in_specs=[pl.BlockSpec(memory_space=pltpu.SMEM),
                  pl.BlockSpec((1, tq, LANES), lambda b, h, i: (b, i, h)),
                  pl.BlockSpec((1, nk, LANES, tk), lambda b, h, i: (b, 0, h, 0)),
                  pl.BlockSpec((1, nk, tk, LANES), lambda b, h, i: (b, 0, 0, h)),
                  pl.BlockSpec((1, LANES), lambda b, h, i: (0, 0))],
        out_specs=pl.BlockSpec((1, tq, LANES), lambda b, h, i: (b, i, h)),
        out_shape=jax.ShapeDtypeStruct((bsz, seq, d), F32),
        scratch_shapes=[pltpu.VMEM((2 * tq, LANES), F32), pltpu.VMEM((2 * tq, LANES), F32),
                        pltpu.VMEM((2 * tq, LANES), F32)],
        compiler_params=_cparams(("parallel", "parallel", "arbitrary")),
        name="attn_prompt",
    )(lam.reshape(1), q.reshape(bsz, seq, d), kt_bf, v_bf.reshape(bsz, nk, tk, d), sub_g.reshape(1, LANES))
    return out.reshape(bsz * seq, d)


def _attn_decode_kernel(pt_ref, lam_ref, q_ref, kn_ref, vn_ref, sg_ref, *rest, nq, nh, pp, lam_init):
    k_refs, v_refs = rest[:pp], rest[pp:2 * pp]
    o_ref, m_ref, l_ref, acc_ref = rest[2 * pp:]
    step = pl.program_id(1)
    rows = q_ref.shape[1] // nh
    rpb = rows // 2
    page = k_refs[0].shape[3]

    @pl.when(step == 0)
    def _():
        m_ref[...] = jnp.full(m_ref.shape, NEG_INF, F32)
        l_ref[...] = jnp.zeros(l_ref.shape, F32)
        acc_ref[...] = jnp.zeros(acc_ref.shape, F32)

    q = q_ref[0] * ATTN_SCALE
    keep = ((_iota(q.shape, 0) % rows) < rpb) == (_iota(q.shape, 1) < AT_HEAD_DIM)
    qs = jnp.where(keep, q, 0.0)
    hrows = [slice(h * rows, (h + 1) * rows) for h in range(nh)]
    bf = lambda x: x.astype(BF16)

    def update(s, pv_fn):
        m_prev = m_ref[...]
        m_new = jnp.maximum(m_prev, jnp.max(s, axis=-1, keepdims=True))
        alpha = jnp.exp(m_prev - m_new)
        p = jnp.exp(s - m_new)
        l_ref[...] = alpha * l_ref[...] + jnp.sum(p, axis=-1, keepdims=True)
        acc_ref[...] = alpha * acc_ref[...] + pv_fn(p)
        m_ref[...] = m_new

    s = jnp.concatenate(
        [jnp.concatenate([_dg(bf(qs[hr]), bf(k_refs[i][0, 0, h * LANES:(h + 1) * LANES, :]), 1, 0)
                          for h, hr in enumerate(hrows)], axis=0) for i in range(pp)], axis=1)

    def pv_pages(p):
        outs = []
        for h, hr in enumerate(hrows):
            o = None
            for i in range(pp):
                vh = bf(v_refs[i][0, 0, pl.ds(h, page, stride=nh), :])
                t = jnp.dot(bf(p[hr, i * page:(i + 1) * page]), vh, preferred_element_type=F32)
                o = t if o is None else o + t
            outs.append(o)
        return jnp.concatenate(outs, axis=0)

    update(s, pv_pages)

    @pl.when(step == pl.num_programs(1) - 1)
    def _():
        nk = kn_ref.shape[2]
        sn = jnp.concatenate([_dg(bf(qs[hr]), bf(kn_ref[0, h]), 1, 1) for h, hr in enumerate(hrows)],
                             axis=0)
        col = _iota(sn.shape, 1)
        causal = (col <= (_iota(sn.shape, 0) % rpb)) & (col < nq)
        update(jnp.where(causal, sn, NEG_INF),
               lambda p: jnp.concatenate([jnp.dot(bf(p[hr]), bf(vn_ref[0, h]), preferred_element_type=F32)
                                          for h, hr in enumerate(hrows)], axis=0))
        o = acc_ref[...] / l_ref[...]
        total = o.shape[0]
        o_ref[0] = _diff_finish(o, pltpu.roll(o, total - rpb, axis=0), lam_ref[0], sg_ref[...], lam_init)


def _attn_decode(q, k, v, cache_k, cache_v, layer, page_table, lam, sub_g, lam_init, bsz, nq, pp):
    d = q.shape[1]
    nh = d // LANES
    n_pages = page_table.shape[1]
    n_layers, n_pool, page = cache_k.shape[:3]
    kt = jnp.transpose(cache_k, (0, 1, 3, 4, 5, 2)).reshape(n_layers, n_pool, d, page)
    vv = cache_v.reshape(n_layers, n_pool, page * nh, LANES)
    rpb = -(-nq // 4) * 4
    heads_first = lambda x: jnp.transpose(x.reshape(bsz, nq, nh, LANES), (0, 2, 1, 3))
    qh = _pad_to(heads_first(q), 2, rpb)
    qh = jnp.concatenate([qh, qh], axis=2).reshape(bsz, nh * 2 * rpb, LANES)
    kh = _pad_to(heads_first(k), 2, 16)
    vh = _pad_to(heads_first(v), 2, 16)
    rows = nh * 2 * rpb
    qspec = pl.BlockSpec((1, rows, LANES), lambda b, s, pt: (b, 0, 0))
    nspec = pl.BlockSpec((1, nh, kh.shape[2], LANES), lambda b, s, pt: (b, 0, 0, 0))

    def page_spec(i, nrows):
        return pl.BlockSpec((1, 1, nrows, LANES), lambda b, s, pt: (layer, pt[b, s * pp + i], 0, 0))

    grid_spec = pltpu.PrefetchScalarGridSpec(
        num_scalar_prefetch=1,
        grid=(bsz, n_pages // pp),
        in_specs=([pl.BlockSpec(memory_space=pltpu.SMEM), qspec, nspec, nspec,
                   pl.BlockSpec((1, LANES), lambda b, s, pt: (0, 0))]
                  + [page_spec(i, d) for i in range(pp)] + [page_spec(i, page * nh) for i in range(pp)]),
        out_specs=qspec,
        scratch_shapes=[pltpu.VMEM((rows, 1), F32), pltpu.VMEM((rows, 1), F32),
                        pltpu.VMEM((rows, LANES), F32)],
    )
    out = pl.pallas_call(
        functools.partial(_attn_decode_kernel, nq=nq, nh=nh, pp=pp, lam_init=lam_init),
        grid_spec=grid_spec,
        out_shape=jax.ShapeDtypeStruct((bsz, rows, LANES), F32),
        compiler_params=_cparams(("parallel", "arbitrary")),
        name="attn_decode",
    )(page_table, lam.reshape(1), qh, kh, vh, sub_g.reshape(1, LANES), *([kt] * pp), *([vv] * pp))
    out = out.reshape(bsz, nh, 2, rpb, LANES)[:, :, 0, :nq]
    return jnp.transpose(out, (0, 2, 1, 3)).reshape(bsz * nq, d)


def _row_tile(m, want):
    t = min(m, want)
    while m % t:
        t //= 2
    return t


def _rwkv_layer(x, shift_prev, s0, norm_g, p):
    bsz, t, d = x.shape
    m = bsz * t
    if t % 256 == 0:
        (r, k, v, wl, a, g), shift_out = _rk_proj_fused(x, shift_prev, norm_g, p, 256)
    else:
        xn = _rmsnorm_rows(x.reshape(m, d), norm_g, _row_tile(m, 512)).reshape(bsz, t, d)
        prev = jnp.concatenate([shift_prev[:, None, :], xn[:, :-1]], axis=1)
        r, k, v, wl, a, g = _rk_proj(xn.reshape(m, d), prev.reshape(m, d), p, _row_tile(m, 256))
        shift_out = xn[:, -1]
    chunk = 64 if t % 64 == 0 else 8
    t_pad = -(-t // chunk) * chunk
    seqs = [u.reshape(bsz, t, d) for u in (r, k, v, wl, a, g)]
    if t_pad != t:
        seqs = [_pad_to(u, 1, chunk) for u in seqs]
    z, s_t = _rk_scan(*seqs, p, s0, chunk, t if t_pad != t else chunk)
    z = z[:, :t].reshape(m, d)
    out = _mm_res(z, p["wo"], x.reshape(m, d), _row_tile(m, 512))
    return out.reshape(bsz, t, d), shift_out, s_t


def _attn_layer(x, pos, norm_g, w_in, qn_g, kn_g, lam, lam_init, sub_g, wo, decode=None):
    bsz, t, d = x.shape
    m = bsz * t
    nh = d // LANES
    x2 = x.reshape(m, d)
    cos_t, sin_t = _rope_tables(pos)
    if decode is None:
        tm = _row_tile(t, 512)
        q, k_t, kt_bf, v, v_bf = _qkv_proj(x2, norm_g, w_in, qn_g, kn_g, cos_t, sin_t, tm, t, True)
        o = _attn_prompt(q, kt_bf, v_bf, lam, sub_g, lam_init, bsz, t, _row_tile(t, 256))
        k_out = jnp.transpose(k_t.reshape(bsz, nh, 2, AT_HEAD_DIM, t), (0, 4, 1, 2, 3))
    else:
        cache_k, cache_v, layer, page_table = decode
        cos_t, sin_t = jnp.tile(cos_t, (bsz, 1)), jnp.tile(sin_t, (bsz, 1))
        q, k, v = _qkv_proj(x2, norm_g, w_in, qn_g, kn_g, cos_t, sin_t, m, t, False)
        pp = _row_tile(page_table.shape[1], 16)
        o = _attn_decode(q, k, v, cache_k, cache_v, layer, page_table, lam, sub_g, lam_init, bsz, t, pp)
        k_out = k.reshape(bsz, t, nh, 2, AT_HEAD_DIM)
    y = _mm_res(o, wo, x2, _row_tile(m, 512)).reshape(bsz, t, d)
    return y, k_out, v.reshape(bsz, t, nh, 2 * AT_HEAD_DIM)


def kernel(x_prompt, x_sample, state_wkv, state_shift, cache_k, cache_v, page_table,
           rk_norm_g, rk_mix, rk_wr, rk_wk, rk_wv, rk_w0, rk_w1, rk_w2, rk_a0, rk_a1, rk_a2,
           rk_g1, rk_g2, rk_k_k, rk_k_a, rk_r_k, rk_lnx_g, rk_lnx_b, rk_wo,
           ffn_norm_g, ffn_w_gu, ffn_w_down,
           at_norm_g, at_w_in, at_q_norm_g, at_k_norm_g, at_lq1, at_lk1, at_lq2, at_lk2,
           at_subln_g, at_wo,
           moe_norm_g, moe_router, moe_w_gu, moe_w_down):
    yp, ys = x_prompt, x_sample
    bp, tp, d = x_prompt.shape
    bs, ts, _ = x_sample.shape
    depth = rk_norm_g.shape[0] + at_norm_g.shape[0]
    past = page_table.shape[1] * cache_k.shape[2]
    pos_p = jnp.arange(tp)
    pos_s = past + jnp.arange(ts)
    nheads = d // RK_HEAD
    p_wkv, p_shift, p_k, p_v = [], [], [], []
    s_wkv, s_shift, s_k, s_v = [], [], [], []
    for i in range(depth):
        j = i // 2
        if i % 2 == 0:
            p = dict(mix=rk_mix[j], wr=rk_wr[j], wk=rk_wk[j], wv=rk_wv[j], w0=rk_w0[j], w1=rk_w1[j],
                     w2=rk_w2[j], a0=rk_a0[j], a1=rk_a1[j], a2=rk_a2[j], g1=rk_g1[j], g2=rk_g2[j],
                     k_k=rk_k_k[j], k_a=rk_k_a[j], r_k=rk_r_k[j].reshape(-1), lnx_g=rk_lnx_g[j],
                     lnx_b=rk_lnx_b[j], wo=rk_wo[j])
            yp, sh, st = _rwkv_layer(yp, jnp.zeros((bp, d), F32),
                                     jnp.zeros((bp, nheads, RK_HEAD, RK_HEAD), F32), rk_norm_g[j], p)
            p_shift.append(sh)
            p_wkv.append(st)
            ys, sh, st = _rwkv_layer(ys, state_shift[j], state_wkv[j], rk_norm_g[j], p)
            s_shift.append(sh)
            s_wkv.append(st)
            yp = _ffn(yp.reshape(bp * tp, d), ffn_norm_g[j], ffn_w_gu[j], ffn_w_down[j],
                      _row_tile(bp * tp, 512)).reshape(bp, tp, d)
            ys = _ffn(ys.reshape(bs * ts, d), ffn_norm_g[j], ffn_w_gu[j], ffn_w_down[j],
                      _row_tile(bs * ts, 512)).reshape(bs, ts, d)
        else:
            lam_init = 0.8 - 0.6 * math.exp(-0.3 * i)
            lam = (jnp.exp(jnp.sum(at_lq1[j] * at_lk1[j])) - jnp.exp(jnp.sum(at_lq2[j] * at_lk2[j]))
                   + lam_init)
            common = (at_norm_g[j], at_w_in[j], at_q_norm_g[j], at_k_norm_g[j], lam, lam_init,
                      at_subln_g[j], at_wo[j])
            yp, k_new, v_new = _attn_layer(yp, pos_p, *common)
            p_k.append(k_new)
            p_v.append(v_new)
            ys, k_new, v_new = _attn_layer(ys, pos_s, *common, decode=(cache_k, cache_v, j, page_table))
            s_k.append(k_new)
            s_v.append(v_new)
            yp = _moe(yp.reshape(bp * tp, d), moe_norm_g[j], moe_router[j], moe_w_gu[j], moe_w_down[j],
                      _row_tile(bp * tp, 1024)).reshape(bp, tp, d)
            ys = _moe(ys.reshape(bs * ts, d), moe_norm_g[j], moe_router[j], moe_w_gu[j], moe_w_down[j],
                      _row_tile(bs * ts, 1024)).reshape(bs, ts, d)
    return (yp, ys, jnp.stack(p_wkv), jnp.stack(p_shift), jnp.stack(p_k), jnp.stack(p_v),
            jnp.stack(s_wkv), jnp.stack(s_shift), jnp.stack(s_k), jnp.stack(s_v))
```

```python
import functools
import math

import jax
import jax.numpy as jnp
from jax import lax
from jax.experimental import pallas as pl
from jax.experimental.pallas import tpu as pltpu

F32 = jnp.float32
BF16 = jnp.bfloat16

RK_HEAD = 64
GN_EPS = 64e-5
AT_HEAD_DIM = 64
ROT_DIM = AT_HEAD_DIM // 4
ROPE_THETA = 500000.0
ATTN_SCALE = AT_HEAD_DIM ** -0.5
LOG2E = 1.4426950408889634
NEG_INF = -1e30
N_EXPERTS = 8
RMS_EPS = 1e-6
LANES = 128
VMEM_LIMIT = 56 * 1024 * 1024


def _cparams(sem):
    return pltpu.CompilerParams(dimension_semantics=sem, vmem_limit_bytes=VMEM_LIMIT)


def _dg(a, b, ca, cb):
    return lax.dot_general(a, b, (((ca,), (cb,)), ((), ())), preferred_element_type=F32)


def _split(x, n):
    parts = []
    for i in range(n):
        p = x.astype(BF16)
        parts.append(p)
        if i + 1 < n:
            x = x - p.astype(F32)
    return parts


def _dot(a, b, ca=1, cb=0, passes=3):
    if passes == 1:
        return _dg(a.astype(BF16), b.astype(BF16), ca, cb)
    if passes == 3:
        ah, al = _split(a, 2)
        bh, bl = _split(b, 2)
        return _dg(ah, bh, ca, cb) + (_dg(ah, bl, ca, cb) + _dg(al, bh, ca, cb))
    ah, am, al = _split(a, 3)
    bh, bm, bl = _split(b, 3)
    return (_dg(ah, bh, ca, cb) + (_dg(ah, bm, ca, cb) + _dg(am, bh, ca, cb))
            + (_dg(am, bm, ca, cb) + _dg(ah, bl, ca, cb) + _dg(al, bh, ca, cb)))


def _dot_exact_rhs(a, b_bf16, ca=1, cb=0):
    ah, al = _split(a, 2)
    return _dg(ah, b_bf16, ca, cb) + _dg(al, b_bf16, ca, cb)


def _dot_exact_lhs(a_bf16, b, ca=1, cb=0):
    bh, bl = _split(b, 2)
    return _dg(a_bf16, bh, ca, cb) + _dg(a_bf16, bl, ca, cb)


def _iota(shape, dim):
    return lax.broadcasted_iota(jnp.int32, shape, dim)


def _rms(x, g):
    return x * lax.rsqrt(jnp.mean(x * x, axis=-1, keepdims=True) + RMS_EPS) * g


def _rmsnorm_kernel(x_ref, g_ref, o_ref):
    o_ref[...] = _rms(x_ref[...], g_ref[...])


def _rmsnorm_rows(x, g, tm):
    m, d = x.shape
    return pl.pallas_call(
        _rmsnorm_kernel,
        grid=(m // tm,),
        in_specs=[pl.BlockSpec((tm, d), lambda i: (i, 0)), pl.BlockSpec((1, d), lambda i: (0, 0))],
        out_specs=pl.BlockSpec((tm, d), lambda i: (i, 0)),
        out_shape=jax.ShapeDtypeStruct((m, d), F32),
        compiler_params=_cparams(("parallel",)),
        name="rmsnorm",
    )(x, g.reshape(1, d))


def _rk_proj_kernel(xn_ref, prev_ref, *refs):
    _rk_proj_body(xn_ref[...], prev_ref[...], *refs)


def _rk_proj_fused_kernel(x_ref, halo_ref, shift_ref, ng_ref, *refs, tiles_per_seq):
    i = pl.program_id(0)
    ng = ng_ref[...]
    xn = _rms(x_ref[...], ng)
    halo = halo_ref[0]
    before = _rms(halo[halo.shape[0] - 1:, :], ng)
    first_prev = jnp.where(i % tiles_per_seq == 0, shift_ref[0], before)
    prev = jnp.where(_iota(xn.shape, 0) == 0, first_prev, pltpu.roll(xn, 1, axis=0))
    tail = refs[-1]
    tail[0] = xn[xn.shape[0] - tail.shape[1]:, :]
    _rk_proj_body(xn, prev, *refs[:-1])


def _rk_proj_body(xn, prev, mix_ref, wr_ref, wk_ref, wv_ref, w1_ref, w2_ref, w0_ref,
                  a1_ref, a2_ref, a0_ref, g1_ref, g2_ref,
                  r_ref, k_ref, v_ref, wl_ref, a_ref, g_ref):
    dx = prev - xn
    mix = mix_ref[...]

    def mixed(c):
        return (xn + dx * mix[c:c + 1, :]).astype(BF16)

    def mm(x, w_ref):
        return jnp.dot(x, w_ref[...], preferred_element_type=F32)

    r_ref[...] = mm(mixed(0), wr_ref)
    k_ref[...] = mm(mixed(2), wk_ref)
    v_ref[...] = mm(mixed(3), wv_ref)
    z = w0_ref[...] + mm(jnp.tanh(mm(mixed(1), w1_ref)).astype(BF16), w2_ref)
    wl_ref[...] = -jax.nn.softplus(-z) - 0.5
    a_ref[...] = jax.nn.sigmoid(a0_ref[...] + mm(mm(mixed(4), a1_ref).astype(BF16), a2_ref))
    g_ref[...] = mm(jax.nn.sigmoid(mm(mixed(5), g1_ref)).astype(BF16), g2_ref)


def _pad_to(x, axis, mult):
    pad = (-x.shape[axis]) % mult
    if pad == 0:
        return x
    widths = [(0, 0)] * x.ndim
    widths[axis] = (0, pad)
    return jnp.pad(x, widths)


def _rk_proj_weights(p, d):
    bf = lambda w: w.astype(BF16)
    w1, w2 = bf(_pad_to(p["w1"], 1, LANES)), bf(_pad_to(p["w2"], 0, LANES))
    a1, a2 = bf(_pad_to(p["a1"], 1, LANES)), bf(_pad_to(p["a2"], 0, LANES))
    g1, g2 = bf(_pad_to(p["g1"], 1, LANES)), bf(_pad_to(p["g2"], 0, LANES))
    return (p["mix"], bf(p["wr"]), bf(p["wk"]), bf(p["wv"]), w1, w2, p["w0"].reshape(1, d),
            a1, a2, p["a0"].reshape(1, d), g1, g2)


def _rk_proj(xn, prev, p, tm):
    m, d = xn.shape
    row = pl.BlockSpec((tm, d), lambda i: (i, 0))
    full = lambda a: pl.BlockSpec(a.shape, lambda i: (0,) * a.ndim)
    weights = _rk_proj_weights(p, d)
    return pl.pallas_call(
        _rk_proj_kernel,
        grid=(m // tm,),
        in_specs=[row, row] + [full(a) for a in weights],
        out_specs=[row] * 6,
        out_shape=[jax.ShapeDtypeStruct((m, d), F32)] * 6,
        compiler_params=_cparams(("parallel",)),
        name="rk_proj",
    )(xn, prev, *weights)


SUBLANES = 8


def _rk_proj_fused(x, shift_prev, norm_g, p, tm):
    bsz, t, d = x.shape
    m = bsz * t
    nt = t // tm
    row = pl.BlockSpec((tm, d), lambda i: (i, 0))
    full = lambda a: pl.BlockSpec(a.shape, lambda i: (0,) * a.ndim)
    weights = _rk_proj_weights(p, d)
    halo = pl.BlockSpec((1, SUBLANES, d), lambda i: (jnp.maximum(i * (tm // SUBLANES) - 1, 0), 0, 0))
    outs = pl.pallas_call(
        functools.partial(_rk_proj_fused_kernel, tiles_per_seq=nt),
        grid=(m // tm,),
        in_specs=[row, halo, pl.BlockSpec((1, 1, d), lambda i: (i // nt, 0, 0)),
                  pl.BlockSpec((1, d), lambda i: (0, 0))] + [full(a) for a in weights],
        out_specs=[row] * 6 + [pl.BlockSpec((1, SUBLANES, d), lambda i: (i // nt, 0, 0))],
        out_shape=[jax.ShapeDtypeStruct((m, d), F32)] * 6 + [jax.ShapeDtypeStruct((bsz, SUBLANES, d), F32)],
        compiler_params=_cparams(("arbitrary",)),
        name="rk_proj_fused",
    )(x.reshape(m, d), x.reshape(m // SUBLANES, SUBLANES, d), shift_prev.reshape(bsz, 1, d),
      norm_g.reshape(1, d), *weights)
    return outs[:6], outs[6][:, SUBLANES - 1]


def _rk_scan_kernel(r_ref, k_ref, v_ref, wl_ref, a_ref, g_ref, kk_ref, ka_ref, rk_ref, lg_ref, lb_ref,
                    s0_ref, z_ref, sT_ref, s_scr, *, chunk, t_valid, npair, passes, state_passes):
    c = pl.program_id(2)
    C = chunk
    C2 = 2 * C

    @pl.when(c == 0)
    def _():
        s_scr[...] = s0_ref[0]

    lane = _iota((C, LANES), 1)
    head0 = lane < RK_HEAD
    same_head = ((_iota((LANES, LANES), 0) < RK_HEAD) == (_iota((LANES, LANES), 1) < RK_HEAD))
    bd_ones = same_head.astype(BF16)
    eye = _iota((LANES, LANES), 0) == _iota((LANES, LANES), 1)
    tri_incl = (_iota((C, C), 0) >= _iota((C, C), 1)).astype(BF16)
    ri = _iota((C2, C2), 0)
    ci = _iota((C2, C2), 1)
    same_blk = (ri < C) == (ci < C)
    strict = same_blk & (ri > ci)
    incl = same_blk & (ri >= ci)
    eye2 = jnp.where(ri == ci, 1.0, 0.0)
    valid = _iota((C, LANES), 0) < t_valid
    dp = functools.partial(_dot, passes=passes)
    ds = functools.partial(_dot, passes=state_passes)
    inv_n = 1.0 / RK_HEAD

    def stack(x):
        return jnp.concatenate([jnp.where(head0, x, 0.0), jnp.where(head0, 0.0, x)], axis=0)

    def fold(x):
        return x[:C] + x[C:]

    def each(fn, *lists):
        return [fn(*xs) for xs in zip(*lists)]

    cat0 = lambda *xs: jnp.concatenate(xs, axis=0)
    cat1 = lambda *xs: jnp.concatenate(xs, axis=1)
    lanes = [slice(pr * LANES, (pr + 1) * LANES) for pr in range(npair)]
    r = [r_ref[0, :, ln] for ln in lanes]
    k = [k_ref[0, :, ln] for ln in lanes]
    v = [v_ref[0, :, ln] for ln in lanes]
    a = [a_ref[0, :, ln] for ln in lanes]
    logw = [-jnp.exp(wl_ref[0, :, ln]) for ln in lanes]
    kk = [ki * kk_ref[:, ln] for ki, ln in zip(k, lanes)]
    k2 = [ki * (1.0 + (ai - 1.0) * ka_ref[:, ln]) for ki, ai, ln in zip(k, a, lanes)]
    sums = [_dot_exact_rhs(cat0(kki * kki, ri_ * k2i * rk_ref[:, ln]), bd_ones)
            for kki, ri_, k2i, ln in zip(kk, r, k2, lanes)]
    kk = each(lambda kki, si: kki / jnp.maximum(jnp.sqrt(si[:C]), 1e-12), kk, sums)
    bonus = each(lambda si, vi: si[C:] * vi, sums, v)
    b = each(lambda kki, ai: kki * ai, kk, a)
    if t_valid < C:
        logw = [jnp.where(valid, x, 0.0) for x in logw]
        b = [jnp.where(valid, x, 0.0) for x in b]
        k2 = [jnp.where(valid, x, 0.0) for x in k2]

    cum = [_dot_exact_lhs(tri_incl, x) for x in logw]
    gam = [jnp.exp(x) for x in cum]
    inv_gam = [jnp.exp(-x) for x in cum]
    gam_prev = each(lambda cu, lw: jnp.exp(cu - lw), cum, logw)
    g_last = [x[C - 1:C, :] for x in gam]

    bt = each(lambda x, ig: x * ig, b, inv_gam)
    kt = each(lambda x, ig: x * ig, k2, inv_gam)
    ats = each(lambda kki, gp: stack(-kki * gp), kk, gam_prev)
    bts, kts, vs = [stack(x) for x in bt], [stack(x) for x in kt], [stack(x) for x in v]
    rts = each(lambda ri_, gi: stack(ri_ * gi), r, gam)
    if C2 % LANES == 0:
        gram = each(lambda at_, rt_, bt_, kt_: dp(cat0(at_, rt_), cat0(bt_, kt_), 1, 1), ats, rts, bts, kts)
        g_ab, g_ak = [x[:C2, :C2] for x in gram], [x[:C2, C2:] for x in gram]
        g_rb, g_rk = [x[C2:, :C2] for x in gram], [x[C2:, C2:] for x in gram]
    else:
        g_ab, g_ak = each(lambda x, y_: dp(x, y_, 1, 1), ats, bts), each(lambda x, y_: dp(x, y_, 1, 1), ats, kts)
        g_rb, g_rk = each(lambda x, y_: dp(x, y_, 1, 1), rts, bts), each(lambda x, y_: dp(x, y_, 1, 1), rts, kts)
    a_ak = [jnp.where(strict, x, 0.0) for x in g_ak]
    m_rb = [jnp.where(incl, x, 0.0) for x in g_rb]
    m_rk = [jnp.where(incl, x, 0.0) for x in g_rk]

    x = [jnp.where(strict, g, 0.0) for g in g_ab]
    t_inv = [eye2 + xi for xi in x]
    for _ in range(int(math.log2(C)) - 1):
        x = [dp(xi, xi) for xi in x]
        t_inv = each(lambda ti, xi: ti + dp(ti, xi), t_inv, x)

    ws = each(dp, a_ak, vs)
    tp = each(lambda ti, at_, wi: dp(ti, cat1(at_, wi)), t_inv, ats, ws)
    mq = each(dp, m_rb, tp)
    mv = each(dp, m_rk, vs)
    q1 = each(lambda rt_, m_: fold(rt_ + m_[:, :LANES]), rts, mq)
    q2 = each(lambda m_, mv_: fold(m_[:, LANES:] + mv_), mq, mv)
    p1 = [fold(t[:, :LANES]) for t in tp]
    p2 = [fold(t[:, LANES:]) for t in tp]

    s = [s_scr[pr] for pr in range(npair)]
    y = each(lambda q1_, s_, q2_: ds(q1_, s_, 1, 1) + q2_, q1, s, q2)
    bp = each(lambda x_, gl: x_ * gl, bt, g_last)
    kp = each(lambda x_, gl: x_ * gl, kt, g_last)
    gm = each(lambda gl, p1_, bp_: jnp.where(eye, gl, 0.0) + jnp.where(same_head, dp(p1_, bp_, 0, 0), 0.0),
              g_last, p1, bp)
    hm = each(lambda p2_, v_, bp_, kp_: jnp.where(same_head, dp(cat0(p2_, v_), cat0(bp_, kp_), 0, 0), 0.0),
              p2, v, bp, kp)
    s_new = each(lambda s_, gm_, hm_: ds(s_, gm_) + hm_, s, gm, hm)
    for pr in range(npair):
        s_scr[pr] = s_new[pr]

    mu = [_dot_exact_rhs(yi, bd_ones) * inv_n for yi in y]
    d = each(lambda yi, mi: yi - mi, y, mu)
    var = [_dot_exact_rhs(di * di, bd_ones) * inv_n for di in d]
    for pr, ln in enumerate(lanes):
        yn = d[pr] * lax.rsqrt(var[pr] + GN_EPS) * lg_ref[:, ln] + lb_ref[:, ln]
        z_ref[0, :, ln] = (yn + bonus[pr]) * g_ref[0, :, ln]

    @pl.when(c == pl.num_programs(2) - 1)
    def _():
        sT_ref[0] = s_scr[...]


def _pair_state(s):
    bsz, h, n, _ = s.shape
    s = s.reshape(bsz, h // 2, 2, n, n)
    z = jnp.zeros_like(s[:, :, 0])
    top = jnp.concatenate([s[:, :, 0], z], axis=-1)
    bot = jnp.concatenate([z, s[:, :, 1]], axis=-1)
    return jnp.concatenate([top, bot], axis=-2)


def _unpair_state(sp):
    n = sp.shape[-1] // 2
    s = jnp.stack([sp[:, :, :n, :n], sp[:, :, n:, n:]], axis=2)
    return s.reshape(sp.shape[0], sp.shape[1] * 2, n, n)


def _rk_scan(r, k, v, wl, a, g, p, s0, chunk, t_valid, npair=8, passes=1, state_passes=3):
    bsz, t, d = r.shape
    width = npair * LANES
    ngroup = d // width
    nchunk = t // chunk
    tok = pl.BlockSpec((1, chunk, width), lambda b, h, c: (b, c, h))
    par = pl.BlockSpec((1, width), lambda b, h, c: (0, h))
    st = pl.BlockSpec((1, npair, LANES, LANES), lambda b, h, c: (b, h, 0, 0))
    vec = lambda x: x.reshape(1, d)
    z, s_t = pl.pallas_call(
        functools.partial(_rk_scan_kernel, chunk=chunk, t_valid=t_valid, npair=npair, passes=passes,
                          state_passes=state_passes),
        grid=(bsz, ngroup, nchunk),
        in_specs=[tok] * 6 + [par] * 5 + [st],
        out_specs=[tok, st],
        out_shape=[jax.ShapeDtypeStruct((bsz, t, d), F32),
                   jax.ShapeDtypeStruct((bsz, d // LANES, LANES, LANES), F32)],
        scratch_shapes=[pltpu.VMEM((npair, LANES, LANES), F32)],
        compiler_params=_cparams(("parallel", "parallel", "arbitrary")),
        name="rk_scan",
    )(r, k, v, wl, a, g, vec(p["k_k"]), vec(p["k_a"]), vec(p["r_k"]), vec(p["lnx_g"]), vec(p["lnx_b"]),
      _pair_state(s0))
    return z, _unpair_state(s_t)


def _mm_res_kernel(x_ref, w_ref, res_ref, o_ref):
    o_ref[...] = res_ref[...] + jnp.dot(x_ref[...].astype(BF16), w_ref[...], preferred_element_type=F32)


def _mm_res(x, w, res, tm):
    m, kd = x.shape
    n = w.shape[1]
    return pl.pallas_call(
        _mm_res_kernel,
        grid=(m // tm,),
        in_specs=[pl.BlockSpec((tm, kd), lambda i: (i, 0)), pl.BlockSpec((kd, n), lambda i: (0, 0)),
                  pl.BlockSpec((tm, n), lambda i: (i, 0))],
        out_specs=pl.BlockSpec((tm, n), lambda i: (i, 0)),
        out_shape=jax.ShapeDtypeStruct((m, n), F32),
        compiler_params=_cparams(("parallel",)),
        name="mm_res",
    )(x, w.astype(BF16), res)


def _ffn_kernel(x_ref, g_ref, wg_ref, wu_ref, wd_ref, o_ref):
    x = x_ref[...]
    xn = _rms(x, g_ref[...]).astype(BF16)
    gt = jnp.dot(xn, wg_ref[...], preferred_element_type=F32)
    up = jnp.dot(xn, wu_ref[...], preferred_element_type=F32)
    h = (jax.nn.silu(gt) * up).astype(BF16)
    o_ref[...] = x + jnp.dot(h, wd_ref[...], preferred_element_type=F32)


def _ffn(x, g, w_gu, w_down, tm):
    m, d = x.shape
    ff = w_down.shape[0]
    wg, wu = w_gu[:, :ff].astype(BF16), w_gu[:, ff:].astype(BF16)
    full = lambda a: pl.BlockSpec(a.shape, lambda i: (0,) * a.ndim)
    row = pl.BlockSpec((tm, d), lambda i: (i, 0))
    wd = w_down.astype(BF16)
    return pl.pallas_call(
        _ffn_kernel,
        grid=(m // tm,),
        in_specs=[row, pl.BlockSpec((1, d), lambda i: (0, 0)), full(wg), full(wu), full(wd)],
        out_specs=row,
        out_shape=jax.ShapeDtypeStruct((m, d), F32),
        compiler_params=_cparams(("parallel",)),
        name="ffn",
    )(x, g.reshape(1, d), wg, wu, wd)


SLOT_BLOCK = 128


def _moe_route_kernel(x_ref, g_ref, rt_ref, xn_ref, pos_ref, post_ref, gate_ref, cnt_ref):
    tm = x_ref.shape[0]
    xn = _rms(x_ref[...], g_ref[...])
    xn_ref[...] = xn.astype(BF16)
    logits = _dot(xn, rt_ref[...], passes=6)
    lt = logits.T[:N_EXPERTS]
    row = _iota(lt.shape, 0)
    v1 = jnp.max(lt, axis=0, keepdims=True)
    i1 = jnp.min(jnp.where(lt == v1, row, N_EXPERTS), axis=0, keepdims=True)
    rest = jnp.where(row == i1, NEG_INF, lt)
    v2 = jnp.max(rest, axis=0, keepdims=True)
    i2 = jnp.min(jnp.where(rest == v2, row, N_EXPERTS), axis=0, keepdims=True)
    e2 = jnp.exp(v2 - v1)
    w1 = 1.0 / (1.0 + e2)
    w2 = e2 / (1.0 + e2)
    sel1, sel2 = row == i1, row == i2
    mask = sel1 | sel2
    gate_ref[0] = jnp.where(sel1, w1, 0.0) + jnp.where(sel2, w2, 0.0)
    before = (_iota((tm, tm), 0) < _iota((tm, tm), 1)).astype(BF16)
    maskf = jnp.where(mask, 1.0, 0.0)
    rank = _dg(maskf.astype(BF16), before, 1, 0)
    pos = jnp.where(mask, rank, -1.0)
    pos_ref[0] = pos
    filler = jnp.full((LANES - N_EXPERTS, tm), -1.0, F32)
    post_ref[...] = jnp.concatenate([pos, filler], axis=0).T
    cnt = jnp.sum(maskf, axis=1, keepdims=True)
    cnt_ref[0] = jnp.broadcast_to(cnt, (N_EXPERTS, LANES)).astype(jnp.int32)


def _moe_expert_kernel(cnt_ref, x_ref, xn_ref, pos_ref, post_ref, gate_ref, wg_ref, wu_ref, wd_ref, o_ref):
    i = pl.program_id(0)
    e = pl.program_id(1)
    tm = x_ref.shape[0]

    @pl.when(e == 0)
    def _():
        o_ref[...] = x_ref[...]

    pos_row = pos_ref[0, pl.ds(e, 1), :]
    gate_row = gate_ref[0, pl.ds(e, 1), :]
    post = post_ref[...]
    pos_col = jnp.sum(jnp.where(_iota(post.shape, 1) == e, post, 0.0), axis=1, keepdims=True)

    def process(first_slot, nslots):
        base = first_slot.astype(F32)
        pick = pos_row == base + _iota((nslots, tm), 0).astype(F32)
        xe = jnp.dot(pick.astype(BF16), xn_ref[...], preferred_element_type=F32).astype(BF16)
        gt = jnp.dot(xe, wg_ref[0], preferred_element_type=F32)
        up = jnp.dot(xe, wu_ref[0], preferred_element_type=F32)
        h = (jax.nn.silu(gt) * up).astype(BF16)
        ye = jnp.dot(h, wd_ref[0], preferred_element_type=F32)
        gslot = jnp.sum(jnp.where(pick, gate_row, 0.0), axis=1, keepdims=True)
        put = pos_col == base + _iota((tm, nslots), 1).astype(F32)
        o_ref[...] += jnp.dot(put.astype(BF16), (ye * gslot).astype(BF16), preferred_element_type=F32)

    def double_block(blk, carry):
        process(blk * (2 * SLOT_BLOCK), 2 * SLOT_BLOCK)
        return carry

    nblocks = (cnt_ref[i, e] + (SLOT_BLOCK - 1)) // SLOT_BLOCK
    lax.fori_loop(0, nblocks // 2, double_block, 0)

    @pl.when(nblocks % 2 == 1)
    def _():
        process((nblocks - 1) * SLOT_BLOCK, SLOT_BLOCK)


def _moe(x, g, router, w_gu, w_down, tm):
    m, d = x.shape
    ne, ffe = w_down.shape[0], w_down.shape[1]
    nt = m // tm
    wg, wu = w_gu[:, :, :ffe].astype(BF16), w_gu[:, :, ffe:].astype(BF16)
    wd = w_down.astype(BF16)
    rt = _pad_to(router, 1, LANES)
    f = jax.ShapeDtypeStruct
    row1 = pl.BlockSpec((tm, d), lambda i: (i, 0))
    xn, pos, post, gate, cnt = pl.pallas_call(
        _moe_route_kernel,
        grid=(nt,),
        in_specs=[row1, pl.BlockSpec((1, d), lambda i: (0, 0)), pl.BlockSpec(rt.shape, lambda i: (0, 0))],
        out_specs=[row1, pl.BlockSpec((1, ne, tm), lambda i: (i, 0, 0)),
                   pl.BlockSpec((tm, LANES), lambda i: (i, 0)), pl.BlockSpec((1, ne, tm), lambda i: (i, 0, 0)),
                   pl.BlockSpec((1, ne, LANES), lambda i: (i, 0, 0))],
        out_shape=[f((m, d), BF16), f((nt, ne, tm), F32), f((m, LANES), F32), f((nt, ne, tm), F32),
                   f((nt, ne, LANES), jnp.int32)],
        compiler_params=_cparams(("parallel",)),
        name="moe_route",
    )(x, g.reshape(1, d), rt)
    row = pl.BlockSpec((tm, d), lambda i, e, c: (i, 0))
    per_tile = pl.BlockSpec((1, ne, tm), lambda i, e, c: (i, 0, 0))
    grid_spec = pltpu.PrefetchScalarGridSpec(
        num_scalar_prefetch=1,
        grid=(nt, ne),
        in_specs=[row, row, per_tile, pl.BlockSpec((tm, LANES), lambda i, e, c: (i, 0)), per_tile,
                  pl.BlockSpec((1, d, ffe), lambda i, e, c: (e, 0, 0)),
                  pl.BlockSpec((1, d, ffe), lambda i, e, c: (e, 0, 0)),
                  pl.BlockSpec((1, ffe, d), lambda i, e, c: (e, 0, 0))],
        out_specs=row,
    )
    return pl.pallas_call(
        _moe_expert_kernel,
        grid_spec=grid_spec,
        out_shape=f((m, d), F32),
        compiler_params=_cparams(("parallel", "arbitrary")),
        name="moe",
    )(cnt[:, :, 0], x, xn, pos, post, gate, wg, wu, wd)


def _rope_tables(pos):
    inv = ROPE_THETA ** (-jnp.arange(0, ROT_DIM, 2, dtype=F32) / ROT_DIM)
    ang = pos.astype(F32)[:, None] * inv[None, :]
    cos, sin = jnp.cos(ang), jnp.sin(ang)
    ones = jnp.ones((pos.shape[0], AT_HEAD_DIM - ROT_DIM), F32)
    cos_g = jnp.concatenate([cos, cos, ones], axis=1)
    sin_g = jnp.concatenate([-sin, sin, 0.0 * ones], axis=1)
    return jnp.tile(cos_g, (1, 2)), jnp.tile(sin_g, (1, 2))


def _norm_rope(x, gain, cos, sin):
    lane = _iota(x.shape, 1)
    first = lane < AT_HEAD_DIM
    sq = x * x
    s0 = jnp.sum(jnp.where(first, sq, 0.0), axis=-1, keepdims=True)
    s1 = jnp.sum(jnp.where(first, 0.0, sq), axis=-1, keepdims=True)
    ms = jnp.where(first, s0, s1) * (1.0 / AT_HEAD_DIM)
    y = x * lax.rsqrt(ms + RMS_EPS) * gain
    half = ROT_DIM // 2
    in_lo = (lane % AT_HEAD_DIM) < half
    partner = jnp.where(in_lo, pltpu.roll(y, LANES - half, axis=1), pltpu.roll(y, half, axis=1))
    return y * cos + partner * sin


def _qkv_kernel(x_ref, g_ref, w_ref, qg_ref, kg_ref, cos_ref, sin_ref, *outs, transpose_k):
    j = pl.program_id(1)
    xn_scr = outs[-1]
    nh = x_ref.shape[1] // LANES

    @pl.when(j == 0)
    def _():
        xn_scr[...] = _rms(x_ref[...], g_ref[...]).astype(BF16)

    y = jnp.dot(xn_scr[...], w_ref[...], preferred_element_type=F32)
    heads = [slice(h * LANES, (h + 1) * LANES) for h in range(nh)]

    @pl.when(j == 0)
    def _():
        for hs in heads:
            outs[0][:, hs] = _norm_rope(y[:, hs], qg_ref[...], cos_ref[...], sin_ref[...])

    @pl.when(j == 1)
    def _():
        for hs in heads:
            kh = _norm_rope(y[:, hs], kg_ref[...], cos_ref[...], sin_ref[...])
            if transpose_k:
                kt = kh.T
                outs[1][0, hs, :] = kt
                outs[2][0, 0, hs, :] = kt.astype(BF16)
            else:
                outs[1][:, hs] = kh

    @pl.when(j == 2)
    def _():
        if transpose_k:
            outs[3][...] = y
            outs[4][...] = y.astype(BF16)
        else:
            outs[2][...] = y


def _qkv_proj(x, g, w_in, qn_g, kn_g, cos_t, sin_t, tm, seq, transpose_k):
    m, d = x.shape
    nt = cos_t.shape[0] // tm
    row = pl.BlockSpec((tm, d), lambda i, j: (i, 0))
    vec = pl.BlockSpec((1, LANES), lambda i, j: (0, 0))
    tab = pl.BlockSpec((tm, LANES), lambda i, j: (i % nt, 0))
    f = jax.ShapeDtypeStruct
    if transpose_k:
        bsz, ns = m // seq, seq // tm
        out_shape = [f((m, d), F32), f((bsz, d, seq), F32), f((bsz, ns, d, tm), BF16), f((m, d), F32),
                     f((m, d), BF16)]
        out_specs = [row, pl.BlockSpec((1, d, tm), lambda i, j: (i // ns, 0, i % ns)),
                     pl.BlockSpec((1, 1, d, tm), lambda i, j: (i // ns, i % ns, 0, 0)), row, row]
    else:
        out_shape = [f((m, d), F32)] * 3
        out_specs = [row] * 3
    return pl.pallas_call(
        functools.partial(_qkv_kernel, transpose_k=transpose_k),
        grid=(m // tm, 3),
        in_specs=[row, pl.BlockSpec((1, d), lambda i, j: (0, 0)), pl.BlockSpec((d, d), lambda i, j: (0, j)),
                  vec, vec, tab, tab],
        out_specs=out_specs,
        out_shape=out_shape,
        scratch_shapes=[pltpu.VMEM((tm, d), BF16)],
        compiler_params=_cparams(("parallel", "arbitrary")),
        name="qkv_proj",
    )(x, g.reshape(1, d), w_in.astype(BF16), jnp.tile(qn_g, 2).reshape(1, LANES),
      jnp.tile(kn_g, 2).reshape(1, LANES), cos_t, sin_t)


def _diff_finish(o1, o2, lam, sub_g, lam_init):
    o = o1 - lam * o2
    o = o * lax.rsqrt(jnp.mean(o * o, axis=-1, keepdims=True) + RMS_EPS) * sub_g
    return o * (1.0 - lam_init)


def _branch_rows(q):
    first = _iota(q.shape, 1) < AT_HEAD_DIM
    return jnp.concatenate([jnp.where(first, q, 0.0), jnp.where(first, 0.0, q)], axis=0)


def _attn_prompt_kernel(lam_ref, q_ref, kt_ref, v_ref, sg_ref, o_ref, m_ref, l_ref, acc_ref,
                        *, tq, tk, lam_init):
    qi = pl.program_id(2)
    qs = _branch_rows(q_ref[0] * (ATTN_SCALE * LOG2E)).astype(BF16)
    m_ref[...] = jnp.full(m_ref.shape, NEG_INF, F32)
    l_ref[...] = jnp.zeros(l_ref.shape, F32)
    acc_ref[...] = jnp.zeros(acc_ref.shape, F32)

    def scores(j, diag=None):
        if diag is None:
            return _dg(qs, kt_ref[0, j], 1, 0)
        s = _dg(qs, kt_ref[0, j, :, diag * tq:(diag + 1) * tq], 1, 0)
        return jnp.where(_iota(s.shape, 1) <= (_iota(s.shape, 0) % tq), s, NEG_INF)

    def update(s, v):
        m_prev = m_ref[...]
        m_new = jnp.maximum(m_prev, jnp.max(s, axis=-1, keepdims=True))
        alpha = jnp.exp2(m_prev - m_new)
        p = jnp.exp2(s - jnp.concatenate([m_new] * (s.shape[1] // LANES), axis=1))
        l_ref[...] = alpha * l_ref[...] + jnp.sum(p, axis=-1, keepdims=True)
        acc_ref[...] = alpha * acc_ref[...] + jnp.dot(p.astype(BF16), v, preferred_element_type=F32)
        m_ref[...] = m_new

    j_last = (qi * tq + (tq - 1)) // tk

    def run(j0, n):
        ss = [scores(j0 + u) for u in range(n)]
        for u in range(n):
            update(ss[u], v_ref[0, j0 + u])

    def body(jj, carry):
        run(4 * jj, 4)
        return carry

    lax.fori_loop(0, j_last // 4, body, 0)
    done = (j_last // 4) * 4

    @pl.when(j_last - done >= 2)
    def _():
        run(done, 2)

    @pl.when(j_last % 2 == 1)
    def _():
        run(j_last - 1, 1)

    nsub = tk // tq
    for c in range(nsub):
        @pl.when(qi % nsub == c)
        def _(c=c):
            if c:
                update(_dg(qs, kt_ref[0, j_last, :, :c * tq], 1, 0), v_ref[0, j_last, :c * tq, :])
            update(scores(j_last, diag=c), v_ref[0, j_last, c * tq:(c + 1) * tq, :])

    o = acc_ref[...] / l_ref[...]
    o_ref[0] = _diff_finish(o[:tq], o[tq:], lam_ref[0], sg_ref[...], lam_init)


def _attn_prompt(q, kt_bf, v_bf, lam, sub_g, lam_init, bsz, seq, tq):
    d = q.shape[1]
    nh = d // LANES
    nk, tk = kt_bf.shape[1], kt_bf.shape[3]
    out = pl.pallas_call(
        functools.partial(_attn_prompt_kernel, tq=tq, tk=tk, lam_init=lam_init),
        grid=(bsz, nh, seq // tq),
        in_specs=[pl.BlockSpec(memory_space=pltpu.SMEM),
                  pl.BlockSpec((1, tq, LANES), lambda b, h, i: (b, i, h)),
                  pl.BlockSpec((1, nk, LANES, tk), lambda b, h, i: (b, 0, h, 0)),
                  pl.BlockSpec((1, nk, tk, LANES), lambda b, h, i: (b, 0, 0, h)),
                  pl.BlockSpec((1, LANES), lambda b, h, i: (0, 0))],
        out_specs=pl.BlockSpec((1, tq, LANES), lambda b, h, i: (b, i, h)),
        out_shape=jax.ShapeDtypeStruct((bsz, seq, d), F32),
        scratch_shapes=[pltpu.VMEM((2 * tq, LANES), F32), pltpu.VMEM((2 * tq, LANES), F32),
                        pltpu.VMEM((2 * tq, LANES), F32)],
        compiler_params=_cparams(("parallel", "parallel", "arbitrary")),
        name="attn_prompt",
    )(lam.reshape(1), q.reshape(bsz, seq, d), kt_bf, v_bf.reshape(bsz, nk, tk, d), sub_g.reshape(1, LANES))
    return out.reshape(bsz * seq, d)


def _attn_decode_kernel(pt_ref, lam_ref, q_ref, kn_ref, vn_ref, sg_ref, *rest, nq, nh, pp, lam_init):
    k_refs, v_refs = rest[:pp], rest[pp:2 * pp]
    o_ref, m_ref, l_ref, acc_ref = rest[2 * pp:]
    step = pl.program_id(1)
    rows = q_ref.shape[1] // nh
    rpb = rows // 2
    page = k_refs[0].shape[3]

    @pl.when(step == 0)
    def _():
        m_ref[...] = jnp.full(m_ref.shape, NEG_INF, F32)
        l_ref[...] = jnp.zeros(l_ref.shape, F32)
        acc_ref[...] = jnp.zeros(acc_ref.shape, F32)

    q = q_ref[0] * ATTN_SCALE
    keep = ((_iota(q.shape, 0) % rows) < rpb) == (_iota(q.shape, 1) < AT_HEAD_DIM)
    qs = jnp.where(keep, q, 0.0)
    hrows = [slice(h * rows, (h + 1) * rows) for h in range(nh)]
    bf = lambda x: x.astype(BF16)

    def update(s, pv_fn):
        m_prev = m_ref[...]
        m_new = jnp.maximum(m_prev, jnp.max(s, axis=-1, keepdims=True))
        alpha = jnp.exp(m_prev - m_new)
        p = jnp.exp(s - m_new)
        l_ref[...] = alpha * l_ref[...] + jnp.sum(p, axis=-1, keepdims=True)
        acc_ref[...] = alpha * acc_ref[...] + pv_fn(p)
        m_ref[...] = m_new

    s = jnp.concatenate(
        [jnp.concatenate([_dg(bf(qs[hr]), bf(k_refs[i][0, 0, h * LANES:(h + 1) * LANES, :]), 1, 0)
                          for h, hr in enumerate(hrows)], axis=0) for i in range(pp)], axis=1)

    def pv_pages(p):
        outs = []
        for h, hr in enumerate(hrows):
            o = None
            for i in range(pp):
                vh = bf(v_refs[i][0, 0, pl.ds(h, page, stride=nh), :])
                t = jnp.dot(bf(p[hr, i * page:(i + 1) * page]), vh, preferred_element_type=F32)
                o = t if o is None else o + t
            outs.append(o)
        return jnp.concatenate(outs, axis=0)

    update(s, pv_pages)

    @pl.when(step == pl.num_programs(1) - 1)
    def _():
        nk = kn_ref.shape[2]
        sn = jnp.concatenate([_dg(bf(qs[hr]), bf(kn_ref[0, h]), 1, 1) for h, hr in enumerate(hrows)],
                             axis=0)
        col = _iota(sn.shape, 1)
        causal = (col <= (_iota(sn.shape, 0) % rpb)) & (col < nq)
        update(jnp.where(causal, sn, NEG_INF),
               lambda p: jnp.concatenate([jnp.dot(bf(p[hr]), bf(vn_ref[0, h]), preferred_element_type=F32)
                                          for h, hr in enumerate(hrows)], axis=0))
        o = acc_ref[...] / l_ref[...]
        total = o.shape[0]
        o_ref[0] = _diff_finish(o, pltpu.roll(o, total - rpb, axis=0), lam_ref[0], sg_ref[...], lam_init)


def _attn_decode(q, k, v, cache_k, cache_v, layer, page_table, lam, sub_g, lam_init, bsz, nq, pp):
    d = q.shape[1]
    nh = d // LANES
    n_pages = page_table.shape[1]
    n_layers, n_pool, page = cache_k.shape[:3]
    kt = jnp.transpose(cache_k, (0, 1, 3, 4, 5, 2)).reshape(n_layers, n_pool, d, page)
    vv = cache_v.reshape(n_layers, n_pool, page * nh, LANES)
    rpb = -(-nq // 4) * 4
    heads_first = lambda x: jnp.transpose(x.reshape(bsz, nq, nh, LANES), (0, 2, 1, 3))
    qh = _pad_to(heads_first(q), 2, rpb)
    qh = jnp.concatenate([qh, qh], axis=2).reshape(bsz, nh * 2 * rpb, LANES)
    kh = _pad_to(heads_first(k), 2, 16)
    vh = _pad_to(heads_first(v), 2, 16)
    rows = nh * 2 * rpb
    qspec = pl.BlockSpec((1, rows, LANES), lambda b, s, pt: (b, 0, 0))
    nspec = pl.BlockSpec((1, nh, kh.shape[2], LANES), lambda b, s, pt: (b, 0, 0, 0))

    def page_spec(i, nrows):
        return pl.BlockSpec((1, 1, nrows, LANES), lambda b, s, pt: (layer, pt[b, s * pp + i], 0, 0))

    grid_spec = pltpu.PrefetchScalarGridSpec(
        num_scalar_prefetch=1,
        grid=(bsz, n_pages // pp),
        in_specs=([pl.BlockSpec(memory_space=pltpu.SMEM), qspec, nspec, nspec,
                   pl.BlockSpec((1, LANES), lambda b, s, pt: (0, 0))]
                  + [page_spec(i, d) for i in range(pp)] + [page_spec(i, page * nh) for i in range(pp)]),
        out_specs=qspec,
        scratch_shapes=[pltpu.VMEM((rows, 1), F32), pltpu.VMEM((rows, 1), F32),
                        pltpu.VMEM((rows, LANES), F32)],
    )
    out = pl.pallas_call(
        functools.partial(_attn_decode_kernel, nq=nq, nh=nh, pp=pp, lam_init=lam_init),
        grid_spec=grid_spec,
        out_shape=jax.ShapeDtypeStruct((bsz, rows, LANES), F32),
        compiler_params=_cparams(("parallel", "arbitrary")),
        name="attn_decode",
    )(page_table, lam.reshape(1), qh, kh, vh, sub_g.reshape(1, LANES), *([kt] * pp), *([vv] * pp))
    out = out.reshape(bsz, nh, 2, rpb, LANES)[:, :, 0, :nq]
    return jnp.transpose(out, (0, 2, 1, 3)).reshape(bsz * nq, d)


def _row_tile(m, want):
    t = min(m, want)
    while m % t:
        t //= 2
    return t


def _rwkv_layer(x, shift_prev, s0, norm_g, p):
    bsz, t, d = x.shape
    m = bsz * t
    if t % 256 == 0:
        (r, k, v, wl, a, g), shift_out = _rk_proj_fused(x, shift_prev, norm_g, p, 256)
    else:
        xn = _rmsnorm_rows(x.reshape(m, d), norm_g, _row_tile(m, 512)).reshape(bsz, t, d)
        prev = jnp.concatenate([shift_prev[:, None, :], xn[:, :-1]], axis=1)
        r, k, v, wl, a, g = _rk_proj(xn.reshape(m, d), prev.reshape(m, d), p, _row_tile(m, 256))
        shift_out = xn[:, -1]
    chunk = 64 if t % 64 == 0 else 8
    t_pad = -(-t // chunk) * chunk
    seqs = [u.reshape(bsz, t, d) for u in (r, k, v, wl, a, g)]
    if t_pad != t:
        seqs = [_pad_to(u, 1, chunk) for u in seqs]
    z, s_t = _rk_scan(*seqs, p, s0, chunk, t if t_pad != t else chunk)
    z = z[:, :t].reshape(m, d)
    out = _mm_res(z, p["wo"], x.reshape(m, d), _row_tile(m, 512))
    return out.reshape(bsz, t, d), shift_out, s_t


def _attn_layer(x, pos, norm_g, w_in, qn_g, kn_g, lam, lam_init, sub_g, wo, decode=None):
    bsz, t, d = x.shape
    m = bsz * t
    nh = d // LANES
    x2 = x.reshape(m, d)
    cos_t, sin_t = _rope_tables(pos)
    if decode is None:
        tm = _row_tile(t, 512)
        q, k_t, kt_bf, v, v_bf = _qkv_proj(x2, norm_g, w_in, qn_g, kn_g, cos_t, sin_t, tm, t, True)
        o = _attn_prompt(q, kt_bf, v_bf, lam, sub_g, lam_init, bsz, t, _row_tile(t, 512))
        k_out = jnp.transpose(k_t.reshape(bsz, nh, 2, AT_HEAD_DIM, t), (0, 4, 1, 2, 3))
    else:
        cache_k, cache_v, layer, page_table = decode
        cos_t, sin_t = jnp.tile(cos_t, (bsz, 1)), jnp.tile(sin_t, (bsz, 1))
        q, k, v = _qkv_proj(x2, norm_g, w_in, qn_g, kn_g, cos_t, sin_t, m, t, False)
        pp = _row_tile(page_table.shape[1], 16)
        o = _attn_decode(q, k, v, cache_k, cache_v, layer, page_table, lam, sub_g, lam_init, bsz, t, pp)
        k_out = k.reshape(bsz, t, nh, 2, AT_HEAD_DIM)
    y = _mm_res(o, wo, x2, _row_tile(m, 512)).reshape(bsz, t, d)
    return y, k_out, v.reshape(bsz, t, nh, 2 * AT_HEAD_DIM)


def kernel(x_prompt, x_sample, state_wkv, state_shift, cache_k, cache_v, page_table,
           rk_norm_g, rk_mix, rk_wr, rk_wk, rk_wv, rk_w0, rk_w1, rk_w2, rk_a0, rk_a1, rk_a2,
           rk_g1, rk_g2, rk_k_k, rk_k_a, rk_r_k, rk_lnx_g, rk_lnx_b, rk_wo,
           ffn_norm_g, ffn_w_gu, ffn_w_down,
           at_norm_g, at_w_in, at_q_norm_g, at_k_norm_g, at_lq1, at_lk1, at_lq2, at_lk2,
           at_subln_g, at_wo,
           moe_norm_g, moe_router, moe_w_gu, moe_w_down):
    yp, ys = x_prompt, x_sample
    bp, tp, d = x_prompt.shape
    bs, ts, _ = x_sample.shape
    depth = rk_norm_g.shape[0] + at_norm_g.shape[0]
    past = page_table.shape[1] * cache_k.shape[2]
    pos_p = jnp.arange(tp)
    pos_s = past + jnp.arange(ts)
    nheads = d // RK_HEAD
    p_wkv, p_shift, p_k, p_v = [], [], [], []
    s_wkv, s_shift, s_k, s_v = [], [], [], []
    for i in range(depth):
        j = i // 2
        if i % 2 == 0:
            p = dict(mix=rk_mix[j], wr=rk_wr[j], wk=rk_wk[j], wv=rk_wv[j], w0=rk_w0[j], w1=rk_w1[j],
                     w2=rk_w2[j], a0=rk_a0[j], a1=rk_a1[j], a2=rk_a2[j], g1=rk_g1[j], g2=rk_g2[j],
                     k_k=rk_k_k[j], k_a=rk_k_a[j], r_k=rk_r_k[j].reshape(-1), lnx_g=rk_lnx_g[j],
                     lnx_b=rk_lnx_b[j], wo=rk_wo[j])
            yp, sh, st = _rwkv_layer(yp, jnp.zeros((bp, d), F32),
                                     jnp.zeros((bp, nheads, RK_HEAD, RK_HEAD), F32), rk_norm_g[j], p)
            p_shift.append(sh)
            p_wkv.append(st)
            ys, sh, st = _rwkv_layer(ys, state_shift[j], state_wkv[j], rk_norm_g[j], p)
            s_shift.append(sh)
            s_wkv.append(st)
            yp = _ffn(yp.reshape(bp * tp, d), ffn_norm_g[j], ffn_w_gu[j], ffn_w_down[j],
                      _row_tile(bp * tp, 512)).reshape(bp, tp, d)
            ys = _ffn(ys.reshape(bs * ts, d), ffn_norm_g[j], ffn_w_gu[j], ffn_w_down[j],
                      _row_tile(bs * ts, 512)).reshape(bs, ts, d)
        else:
            lam_init = 0.8 - 0.6 * math.exp(-0.3 * i)
            lam = (jnp.exp(jnp.sum(at_lq1[j] * at_lk1[j])) - jnp.exp(jnp.sum(at_lq2[j] * at_lk2[j]))
                   + lam_init)
            common = (at_norm_g[j], at_w_in[j], at_q_norm_g[j], at_k_norm_g[j], lam, lam_init,
                      at_subln_g[j], at_wo[j])
            yp, k_new, v_new = _attn_layer(yp, pos_p, *common)
            p_k.append(k_new)
            p_v.append(v_new)
            ys, k_new, v_new = _attn_layer(ys, pos_s, *common, decode=(cache_k, cache_v, j, page_table))
            s_k.append(k_new)
            s_v.append(v_new)
            yp = _moe(yp.reshape(bp * tp, d), moe_norm_g[j], moe_router[j], moe_w_gu[j], moe_w_down[j],
                      _row_tile(bp * tp, 1024)).reshape(bp, tp, d)
            ys = _moe(ys.reshape(bs * ts, d), moe_norm_g[j], moe_router[j], moe_w_gu[j], moe_w_down[j],
                      _row_tile(bs * ts, 1024)).reshape(bs, ts, d)
    return (yp, ys, jnp.stack(p_wkv), jnp.stack(p_shift), jnp.stack(p_k), jnp.stack(p_v),
            jnp.stack(s_wkv), jnp.stack(s_shift), jnp.stack(s_k), jnp.stack(s_v))
```
